```python
import math
import jax, jax.numpy as jnp
from jax import lax
import numpy as np

D_MODEL = 1024
BATCH = 8
SEQ = 2048
DEPTH = 2

CHUNK = 64
Q_BLOCK = 128
N_BRANCH = 3
BRANCH_WIDTH = D_MODEL // 2
ATT_HEADS = 4
ATT_QK_DIM = BRANCH_WIDTH // (2 * ATT_HEADS)
ATT_V_DIM = 2 * ATT_QK_DIM
ATT_QK_WIDTH = ATT_HEADS * 2 * ATT_QK_DIM
POOL_GROUPS = 4
POOL_WINDOWS = (2, 4, 8, 16)
POOL_GROUP_DIM = BRANCH_WIDTH // POOL_GROUPS
SGU_GROUPS = 4
SGU_BLOCK = 128
SGU_GROUP_DIM = BRANCH_WIDTH // SGU_GROUPS
IN_SPLIT_SIZES = (ATT_QK_WIDTH, ATT_QK_WIDTH, BRANCH_WIDTH, BRANCH_WIDTH,
                  BRANCH_WIDTH, BRANCH_WIDTH,
                  BRANCH_WIDTH, BRANCH_WIDTH, BRANCH_WIDTH)
IN_WIDTH = sum(IN_SPLIT_SIZES)
NORM_EPS = 1e-6

kernel_name = "hybrid_diffattn_pool_sgu_block"


def rms_norm(x, g):
    xf = x.astype(jnp.float32)
    y = xf * lax.rsqrt(jnp.mean(xf * xf, axis=-1, keepdims=True) + NORM_EPS)
    return (y * g.astype(jnp.float32)).astype(x.dtype)


def layer_norm(x, g, b):
    xf = x.astype(jnp.float32)
    mu = jnp.mean(xf, axis=-1, keepdims=True)
    var = jnp.mean(jnp.square(xf - mu), axis=-1, keepdims=True)
    y = (xf - mu) * lax.rsqrt(var + NORM_EPS)
    return (y * g.astype(jnp.float32) + b.astype(jnp.float32)).astype(x.dtype)


def diff_attention(q, k, v, lam):
    B, H, _, S, _ = q.shape
    key_chunk = jnp.arange(S) // CHUNK
    scale = ATT_QK_DIM ** -0.5

    def attend_block(start):
        qb = lax.dynamic_slice_in_dim(q, start, Q_BLOCK, axis=3)
        q_chunk = (start + jnp.arange(Q_BLOCK)) // CHUNK
        mask = key_chunk[None, :] <= q_chunk[:, None]
        s = jnp.einsum('bhmqd,bhmkd->bhmqk', qb, k).astype(jnp.float32) * scale
        s = jnp.where(mask, s, -jnp.inf)
        p = jax.nn.softmax(s, axis=-1)
        a = p[:, :, 0] - lam * p[:, :, 1]
        return jnp.einsum('bhqk,bhkd->bhqd', a.astype(v.dtype), v)

    starts = jnp.arange(S // Q_BLOCK) * Q_BLOCK
    o = lax.map(attend_block, starts)
    return o.transpose(1, 2, 0, 3, 4).reshape(B, H, S, ATT_V_DIM)


def multi_scale_pool(u, w, b, scale):
    B, S, _ = u.shape
    ug = u.reshape(B, S, POOL_GROUPS, POOL_GROUP_DIM)
    uf = ug.astype(jnp.float32)
    cs = jnp.pad(jnp.cumsum(uf, axis=1), ((0, 0), (1, 0), (0, 0), (0, 0)))
    t = jnp.arange(S)[:, None]
    win = jnp.array(POOL_WINDOWS, dtype=jnp.int32)[None, :]
    lo = jnp.maximum(t + 1 - win, 0)
    window_sum = cs[:, 1:] - cs[:, lo, jnp.arange(POOL_GROUPS)[None, :]]
    count = (t + 1 - lo).astype(jnp.float32)[None, :, :, None]
    pooled = (window_sum / count - uf).astype(u.dtype)
    mixed = jnp.einsum('bsgc,gcd->bsgd', pooled, w) + b
    return mixed.reshape(B, S, BRANCH_WIDTH) * scale


def spatial_gating(u, v, ln_g, ln_b, w_s, b_s):
    B, S, _ = u.shape
    nb = S // SGU_BLOCK
    u = jax.nn.gelu(u, approximate=False)
    v = layer_norm(jax.nn.gelu(v, approximate=False), ln_g, ln_b)
    vb = v.reshape(B, nb, SGU_BLOCK, SGU_GROUPS, SGU_GROUP_DIM)
    pos_chunk = jnp.arange(SGU_BLOCK) // CHUNK
    mask = pos_chunk[None, :] <= pos_chunk[:, None]
    w = jnp.where(mask[None], w_s, jnp.zeros_like(w_s))
    mixed = jnp.einsum('gij,bnjgc->bnigc', w, vb) + b_s.T[None, None, :, :, None]
    ub = u.reshape(B, nb, SGU_BLOCK, SGU_GROUPS, SGU_GROUP_DIM)
    return (ub * mixed).reshape(B, S, BRANCH_WIDTH)


def hybrid_layer(x, layer_idx, pre_g, post_g, w_in, lq1, lk1, lq2, lk2, subln_g,
                 pool_w, pool_b, pool_scale, sgu_ln_g, sgu_ln_b, sgu_w, sgu_b,
                 w_branch, w_merge, b_merge, w_out):
    B, S, D = x.shape
    h = rms_norm(x, pre_g)
    z = h @ w_in
    offsets = []
    acc = 0
    for size in IN_SPLIT_SIZES[:-1]:
        acc += size
        offsets.append(acc)
    q, k, v, g_a, p_in, g_b, s_u, s_v, g_c = jnp.split(z, offsets, axis=-1)

    q = q.reshape(B, S, ATT_HEADS, 2, ATT_QK_DIM).transpose(0, 2, 3, 1, 4)
    k = k.reshape(B, S, ATT_HEADS, 2, ATT_QK_DIM).transpose(0, 2, 3, 1, 4)
    v = v.reshape(B, S, ATT_HEADS, ATT_V_DIM).transpose(0, 2, 1, 3)
    lam_init = 0.8 - 0.6 * math.exp(-0.3 * layer_idx)
    f32 = jnp.float32
    lam = (jnp.exp(jnp.sum(lq1.astype(f32) * lk1.astype(f32)))
           - jnp.exp(jnp.sum(lq2.astype(f32) * lk2.astype(f32))) + lam_init)
    o = diff_attention(q, k, v, lam)
    o = rms_norm(o, subln_g) * (1.0 - lam_init)
    y_a = o.transpose(0, 2, 1, 3).reshape(B, S, BRANCH_WIDTH)

    y_b = multi_scale_pool(p_in, pool_w, pool_b, pool_scale)

    y_c = spatial_gating(s_u, s_v, sgu_ln_g, sgu_ln_b, sgu_w, sgu_b)

    ys = jnp.stack([y_a, y_b, y_c], axis=2) * jax.nn.silu(jnp.stack([g_a, g_b, g_c], axis=2))
    branch = jnp.einsum('bsnw,nwd->bsnd', ys, w_branch)
    gates = jax.nn.sigmoid(h @ w_merge + b_merge).reshape(B, S, N_BRANCH, D)
    merged = jnp.sum(gates * branch, axis=2)
    out = merged @ w_out
    return x + rms_norm(out, post_g)


def setup_inputs(seed: int = 0) -> dict:
    key = jax.random.key(seed)
    ks = jax.random.split(key, 24)
    n = jax.random.normal
    L, D, W = DEPTH, D_MODEL, BRANCH_WIDTH
    return {
        "x": n(ks[0], (BATCH, SEQ, D), jnp.float32),
        "pre_norm_g": 1.0 + 0.05 * n(ks[1], (L, D), jnp.float32),
        "post_norm_g": 1.0 + 0.05 * n(ks[2], (L, D), jnp.float32),
        "w_in": n(ks[3], (L, D, IN_WIDTH), jnp.float32) * D ** -0.5,
        "lambda_q1": 0.1 * n(ks[4], (L, ATT_QK_DIM), jnp.float32),
        "lambda_k1": 0.1 * n(ks[5], (L, ATT_QK_DIM), jnp.float32),
        "lambda_q2": 0.1 * n(ks[6], (L, ATT_QK_DIM), jnp.float32),
        "lambda_k2": 0.1 * n(ks[7], (L, ATT_QK_DIM), jnp.float32),
        "attn_subln_g": 1.0 + 0.05 * n(ks[8], (L, ATT_V_DIM), jnp.float32),
        "pool_w": n(ks[9], (L, POOL_GROUPS, POOL_GROUP_DIM, POOL_GROUP_DIM), jnp.float32) * POOL_GROUP_DIM ** -0.5,
        "pool_b": 0.02 * n(ks[10], (L, POOL_GROUPS, POOL_GROUP_DIM), jnp.float32),
        "pool_scale": 1.0 + 0.1 * n(ks[11], (L, W), jnp.float32),
        "sgu_ln_g": 1.0 + 0.05 * n(ks[12], (L, W), jnp.float32),
        "sgu_ln_b": 0.02 * n(ks[13], (L, W), jnp.float32),
        "sgu_w": n(ks[14], (L, SGU_GROUPS, SGU_BLOCK, SGU_BLOCK), jnp.float32) * SGU_BLOCK ** -0.5,
        "sgu_b": 1.0 + 0.05 * n(ks[15], (L, SGU_GROUPS, SGU_BLOCK), jnp.float32),
        "w_branch": n(ks[16], (L, N_BRANCH, W, D), jnp.float32) * W ** -0.5,
        "w_merge": n(ks[17], (L, D, N_BRANCH * D), jnp.float32) * D ** -0.5,
        "b_merge": 0.02 * n(ks[18], (L, N_BRANCH * D), jnp.float32),
        "w_out": n(ks[19], (L, D, D), jnp.float32) * D ** -0.5,
    }


def reference(x, pre_norm_g, post_norm_g, w_in, lambda_q1, lambda_k1, lambda_q2, lambda_k2,
              attn_subln_g, pool_w, pool_b, pool_scale, sgu_ln_g, sgu_ln_b, sgu_w, sgu_b,
              w_branch, w_merge, b_merge, w_out):
    for l in range(DEPTH):
        x = hybrid_layer(x, l, pre_norm_g[l], post_norm_g[l], w_in[l],
                         lambda_q1[l], lambda_k1[l], lambda_q2[l], lambda_k2[l], attn_subln_g[l],
                         pool_w[l], pool_b[l], pool_scale[l], sgu_ln_g[l], sgu_ln_b[l],
                         sgu_w[l], sgu_b[l], w_branch[l], w_merge[l], b_merge[l], w_out[l])
    return x
```

```python
import functools
import math

import jax
import jax.numpy as jnp
from jax import lax
from jax.experimental import pallas as pl
from jax.experimental.pallas import tpu as pltpu

D_MODEL = 1024
SEQ = 2048
DEPTH = 2
CHUNK = 64
WIDTH = D_MODEL // 2
HEADS = 4
QK_DIM = 64
V_DIM = 128
POOL_WINDOWS = (2, 4, 8, 16)
GROUPS = 4
GROUP_DIM = WIDTH // GROUPS
SGU_BLOCK = 128
N_BRANCH = 3
EPS = 1e-6
HALO = 16

LANES = 128
TQ = 256
VMEM_LIMIT_BYTES = 56 * 1024 * 1024

OFF_Q, OFF_K, OFF_V, OFF_GA, OFF_P, OFF_GB, OFF_SU, OFF_SV, OFF_GC = (
    0, 512, 1024, 1536, 2048, 2560, 3072, 3584, 4096)

F32 = jnp.float32
BF16 = jnp.bfloat16


def _dot(a, b):
    return jnp.dot(a, b, preferred_element_type=F32)


def _dot_nt(a, b):
    return lax.dot_general(a, b, (((1,), (1,)), ((), ())), preferred_element_type=F32)


def _silu(x):
    return x / (1.0 + jnp.exp(-x))


def _sigmoid(x):
    return 1.0 / (1.0 + jnp.exp(-x))


def _gelu(x):
    return 0.5 * x * (1.0 + lax.erf(x * math.sqrt(0.5)))


def _lane_tile(x, n):
    return x if n == 1 else jnp.concatenate([x] * n, axis=1)


def _layer_kernel(x_ref, pre_g_ref, post_g_ref, w_in_ref, lq1_ref, lk1_ref, lq2_ref, lk2_ref,
                  subln_ref, pool_w_ref, pool_b_ref, pool_scale_ref, ln_g_ref, ln_b_ref,
                  sgu_w_ref, sgu_bt_ref, w_branch_ref, w_merge_ref, b_merge_ref, w_out_ref,
                  out_ref,
                  q_s, k_s, v_s, m_s, l_s, acc_s, ya_s, ext_s, *, lam_init):
    i = pl.program_id(1)
    x = x_ref[0]

    h = x * lax.rsqrt(jnp.mean(x * x, axis=-1, keepdims=True) + EPS) * pre_g_ref[...]
    hb = h.astype(BF16)

    def proj(off, width=WIDTH):
        return _dot(hb, w_in_ref[:, off:off + width])

    lam = (jnp.exp(jnp.sum(lq1_ref[...] * lk1_ref[...], axis=-1, keepdims=True))
           - jnp.exp(jnp.sum(lq2_ref[...] * lk2_ref[...], axis=-1, keepdims=True))
           + lam_init)

    rows_i = pl.ds(pl.multiple_of(i * TQ, TQ), TQ)
    q = (proj(OFF_Q) * (QK_DIM ** -0.5)).astype(BF16)
    k = proj(OFF_K).astype(BF16)
    v = proj(OFF_V).astype(BF16)
    first_map = lax.broadcasted_iota(jnp.int32, (TQ, LANES), 1) < QK_DIM
    zero = jnp.zeros((TQ, LANES), BF16)
    for hd in range(HEADS):
        cols = slice(hd * LANES, (hd + 1) * LANES)
        q_s[hd] = q[:, cols]
        k_s[2 * hd, rows_i, :] = jnp.where(first_map, k[:, cols], zero)
        k_s[2 * hd + 1, rows_i, :] = jnp.where(first_map, zero, k[:, cols])
        v_s[hd, rows_i, :] = v[:, cols]

    row_chunk = lax.shift_right_logical(lax.broadcasted_iota(jnp.int32, (TQ, TQ), 0), 6)
    col_chunk = lax.shift_right_logical(lax.broadcasted_iota(jnp.int32, (TQ, TQ), 1), 6)
    diag_mask = col_chunk <= row_chunk
    n_rep = TQ // LANES

    def head_body(hd, carry):
        qh = q_s[hd]
        vd = v_s[hd, rows_i, :]
        for mp in range(2):
            s = _dot_nt(qh, k_s[2 * hd + mp, rows_i, :])
            s = jnp.where(diag_mask, s, -jnp.inf)
            mx = jnp.max(s, axis=1, keepdims=True)
            p = jnp.exp(s - mx)
            m_s[mp] = jnp.broadcast_to(mx, (TQ, LANES))
            l_s[mp] = jnp.broadcast_to(jnp.sum(p, axis=1, keepdims=True), (TQ, LANES))
            acc_s[mp] = _dot(p.astype(BF16), vd)

        def kv_body(j, c):
            rows = pl.ds(pl.multiple_of(j * TQ, TQ), TQ)
            vj = v_s[hd, rows, :]
            for mp in range(2):
                s = _dot_nt(qh, k_s[2 * hd + mp, rows, :])
                m_prev = m_s[mp]
                m_next = jnp.maximum(m_prev, jnp.max(s, axis=1, keepdims=True))
                alpha = jnp.exp(m_prev - m_next)
                p = jnp.exp(s - _lane_tile(m_next, n_rep))
                l_s[mp] = alpha * l_s[mp] + jnp.sum(p, axis=1, keepdims=True)
                acc_s[mp] = alpha * acc_s[mp] + _dot(p.astype(BF16), vj)
                m_s[mp] = m_next
            return c

        lax.fori_loop(0, i, kv_body, 0)
        o = acc_s[0] / l_s[0] - lam * (acc_s[1] / l_s[1])
        o = o * lax.rsqrt(jnp.mean(o * o, axis=-1, keepdims=True) + EPS) * subln_ref[...]
        ya_s[hd] = o * (1.0 - lam_init)
        return carry

    lax.fori_loop(0, HEADS, head_body, 0)
    y_a = jnp.concatenate([ya_s[hd] for hd in range(HEADS)], axis=1)

    def branch_out(n, y, gate_off):
        ys = (y * _silu(proj(gate_off))).astype(BF16)
        br = _dot(ys, w_branch_ref[n])
        cols = slice(n * D_MODEL, (n + 1) * D_MODEL)
        gate = _sigmoid(_dot(hb, w_merge_ref[:, cols]) + b_merge_ref[:, cols])
        return gate * br

    merged = branch_out(0, y_a, OFF_GA)

    @pl.when(i == 0)
    def _():
        ext_s[0:HALO, :] = jnp.zeros((HALO, WIDTH), F32)

    ext_s[HALO:HALO + TQ, :] = proj(OFF_P)
    pos = lax.broadcasted_iota(jnp.int32, (TQ, LANES), 0) + (i * TQ + 1)
    mixed = []
    for g, win in enumerate(POOL_WINDOWS):
        cols = slice(g * GROUP_DIM, (g + 1) * GROUP_DIM)
        u = ext_s[HALO:HALO + TQ, cols]
        wsum = u
        for d in range(1, win):
            wsum = wsum + ext_s[HALO - d:HALO - d + TQ, cols]
        count = jnp.minimum(pos, win).astype(F32)
        pooled = (wsum / count - u).astype(BF16)
        mixed.append(_dot(pooled, pool_w_ref[g]))
    y_b = (jnp.concatenate(mixed, axis=1) + pool_b_ref[...]) * pool_scale_ref[...]
    ext_s[0:HALO, :] = ext_s[TQ:TQ + HALO, :]
    merged = merged + branch_out(1, y_b, OFF_GB)

    su = _gelu(proj(OFF_SU))
    sv = _gelu(proj(OFF_SV))
    mu = jnp.mean(sv, axis=-1, keepdims=True)
    cen = sv - mu
    var = jnp.mean(cen * cen, axis=-1, keepdims=True)
    svn = (cen * lax.rsqrt(var + EPS) * ln_g_ref[...] + ln_b_ref[...]).astype(BF16)
    pr = lax.shift_right_logical(lax.broadcasted_iota(jnp.int32, (SGU_BLOCK, SGU_BLOCK), 0), 6)
    pc = lax.shift_right_logical(lax.broadcasted_iota(jnp.int32, (SGU_BLOCK, SGU_BLOCK), 1), 6)
    sgu_mask = pc <= pr
    blocks = []
    for nb in range(TQ // SGU_BLOCK):
        rws = slice(nb * SGU_BLOCK, (nb + 1) * SGU_BLOCK)
        groups = []
        for g in range(GROUPS):
            cols = slice(g * GROUP_DIM, (g + 1) * GROUP_DIM)
            wg = jnp.where(sgu_mask, sgu_w_ref[g], 0.0).astype(BF16)
            bias = jnp.broadcast_to(sgu_bt_ref[:, g:g + 1], (SGU_BLOCK, GROUP_DIM))
            groups.append(_dot(wg, svn[rws, cols]) + bias)
        blocks.append(jnp.concatenate(groups, axis=1))
    y_c = su * jnp.concatenate(blocks, axis=0)
    merged = merged + branch_out(2, y_c, OFF_GC)

    out = _dot(merged.astype(BF16), w_out_ref[...])
    out = out * lax.rsqrt(jnp.mean(out * out, axis=-1, keepdims=True) + EPS) * post_g_ref[...]
    out_ref[0] = x + out


def _full(shape):
    return pl.BlockSpec(shape, lambda b, i: (0,) * len(shape))


def _layer(x, layer_idx, pre_g, post_g, w_in, lq1, lk1, lq2, lk2, subln_g, pool_w, pool_b,
           pool_scale, ln_g, ln_b, sgu_w, sgu_b, w_branch, w_merge, b_merge, w_out):
    batch, seq, d = x.shape
    assert d == D_MODEL and seq == SEQ and seq % TQ == 0 and TQ % SGU_BLOCK == 0
    lam_init = 0.8 - 0.6 * math.exp(-0.3 * layer_idx)
    row = lambda a: a.reshape(1, -1)
    operands = (
        x, row(pre_g), row(post_g), w_in.astype(BF16), row(lq1), row(lk1), row(lq2), row(lk2),
        row(subln_g), pool_w.astype(BF16), row(pool_b), row(pool_scale), row(ln_g), row(ln_b),
        sgu_w, sgu_b.T, w_branch.astype(BF16), w_merge.astype(BF16), row(b_merge),
        w_out.astype(BF16))
    x_spec = pl.BlockSpec((1, TQ, D_MODEL), lambda b, i: (b, i, 0))
    in_specs = [x_spec] + [_full(a.shape) for a in operands[1:]]
    scratch = [
        pltpu.VMEM((HEADS, TQ, LANES), BF16),
        pltpu.VMEM((2 * HEADS, SEQ, LANES), BF16),
        pltpu.VMEM((HEADS, SEQ, LANES), BF16),
        pltpu.VMEM((2, TQ, LANES), F32),
        pltpu.VMEM((2, TQ, LANES), F32),
        pltpu.VMEM((2, TQ, LANES), F32),
        pltpu.VMEM((HEADS, TQ, LANES), F32),
        pltpu.VMEM((TQ + HALO, WIDTH), F32),
    ]
    return pl.pallas_call(
        functools.partial(_layer_kernel, lam_init=lam_init),
        grid=(batch, seq // TQ),
        in_specs=in_specs,
        out_specs=x_spec,
        out_shape=jax.ShapeDtypeStruct(x.shape, x.dtype),
        scratch_shapes=scratch,
        compiler_params=pltpu.CompilerParams(
            dimension_semantics=("arbitrary", "arbitrary"),
            vmem_limit_bytes=VMEM_LIMIT_BYTES),
        name=f"hybrid_layer_{layer_idx}",
    )(*operands)


def kernel(x, pre_norm_g, post_norm_g, w_in, lambda_q1, lambda_k1, lambda_q2, lambda_k2, attn_subln_g, pool_w, pool_b, pool_scale, sgu_ln_g, sgu_ln_b, sgu_w, sgu_b, w_branch, w_merge, b_merge, w_out):
    for l in range(DEPTH):
        x = _layer(x, l, pre_norm_g[l], post_norm_g[l], w_in[l], lambda_q1[l], lambda_k1[l],
                   lambda_q2[l], lambda_k2[l], attn_subln_g[l], pool_w[l], pool_b[l],
                   pool_scale[l], sgu_ln_g[l], sgu_ln_b[l], sgu_w[l], sgu_b[l], w_branch[l],
                   w_merge[l], b_merge[l], w_out[l])
    return x
```

```python
import functools
import math

import jax
import jax.numpy as jnp
from jax import lax
from jax.experimental import pallas as pl
from jax.experimental.pallas import tpu as pltpu

D_MODEL = 1024
SEQ = 2048
DEPTH = 2
CHUNK = 64
WIDTH = D_MODEL // 2
HEADS = 4
QK_DIM = 64
V_DIM = 128
POOL_WINDOWS = (2, 4, 8, 16)
GROUPS = 4
GROUP_DIM = WIDTH // GROUPS
SGU_BLOCK = 128
N_BRANCH = 3
EPS = 1e-6
HALO = 16

LANES = 128
TQ = 256
VMEM_LIMIT_BYTES = 56 * 1024 * 1024

OFF_Q, OFF_K, OFF_V, OFF_GA, OFF_P, OFF_GB, OFF_SU, OFF_SV, OFF_GC = (
    0, 512, 1024, 1536, 2048, 2560, 3072, 3584, 4096)

F32 = jnp.float32
BF16 = jnp.bfloat16


def _dot(a, b):
    return jnp.dot(a, b, preferred_element_type=F32)


def _dot_nt(a, b):
    return lax.dot_general(a, b, (((1,), (1,)), ((), ())), preferred_element_type=F32)


def _dot_tn(a, b):
    return lax.dot_general(a, b, (((0,), (0,)), ((), ())), preferred_element_type=F32)


def _silu(x):
    return x / (1.0 + jnp.exp(-x))


def _sigmoid(x):
    return 1.0 / (1.0 + jnp.exp(-x))


def _gelu(x):
    return 0.5 * x * (1.0 + lax.erf(x * math.sqrt(0.5)))


def _layer_kernel(x_ref, pre_g_ref, post_g_ref, w_in_ref, w_vgt_ref, lq1_ref, lk1_ref, lq2_ref,
                  lk2_ref, subln_ref, pool_w_ref, pool_b_ref, pool_scale_ref, ln_g_ref, ln_b_ref,
                  sgu_w_ref, sgu_bt_ref, w_branch_ref, w_merge_ref, b_merge_ref, w_out_ref,
                  out_ref,
                  q_s, k_s, vt_s, m_s, l_s, acc_s, ext_s, *, lam_init):
    i = pl.program_id(1)
    x = x_ref[0]

    h = x * lax.rsqrt(jnp.mean(x * x, axis=-1, keepdims=True) + EPS) * pre_g_ref[...]
    hb = h.astype(BF16)

    def proj(off, width=WIDTH):
        return _dot(hb, w_in_ref[:, off:off + width])

    lam = (jnp.exp(jnp.sum(lq1_ref[...] * lk1_ref[...], axis=-1, keepdims=True))
           - jnp.exp(jnp.sum(lq2_ref[...] * lk2_ref[...], axis=-1, keepdims=True))
           + lam_init)

    rows_i = pl.ds(pl.multiple_of(i * TQ, TQ), TQ)
    q = (proj(OFF_Q) * (QK_DIM ** -0.5)).astype(BF16)
    k = proj(OFF_K).astype(BF16)
    vgt = _dot_nt(w_vgt_ref[...], hb)
    vt_s[i] = vgt[:WIDTH].astype(BF16)
    gate_a_t = vgt[WIDTH:]
    first_map = lax.broadcasted_iota(jnp.int32, (TQ, LANES), 1) < QK_DIM
    zero = jnp.zeros((TQ, LANES), BF16)
    for hd in range(HEADS):
        cols = slice(hd * LANES, (hd + 1) * LANES)
        q_s[hd, 0:TQ, :] = jnp.where(first_map, q[:, cols], zero)
        q_s[hd, TQ:2 * TQ, :] = jnp.where(first_map, zero, q[:, cols])
        k_s[hd, rows_i, :] = k[:, cols]

    def scores_t(hd, j):
        rows = pl.ds(pl.multiple_of(j * TQ, TQ), TQ)
        return _dot_nt(k_s[hd, rows, :], q_s[hd])

    def values_t(hd, j):
        return vt_s[j, hd * V_DIM:(hd + 1) * V_DIM, :]

    key_chunk = lax.shift_right_logical(lax.broadcasted_iota(jnp.int32, (TQ, 2 * TQ), 0), 6)
    qry_chunk = lax.shift_right_logical(
        lax.broadcasted_iota(jnp.int32, (TQ, 2 * TQ), 1) & (TQ - 1), 6)
    visible = key_chunk <= qry_chunk
    for hd in range(HEADS):
        s = jnp.where(visible, scores_t(hd, i), -jnp.inf)
        mx = jnp.max(s, axis=0, keepdims=True)
        p = jnp.exp(s - mx)
        m_s[hd] = mx
        l_s[hd] = jnp.sum(p, axis=0, keepdims=True)
        acc_s[hd] = _dot(values_t(hd, i), p.astype(BF16))

    def kv_body(j, c):
        for hd in range(HEADS):
            s = scores_t(hd, j)
            m_prev = m_s[hd]
            m_next = jnp.maximum(m_prev, jnp.max(s, axis=0, keepdims=True))
            alpha = jnp.exp(m_prev - m_next)
            p = jnp.exp(s - m_next)
            l_s[hd] = alpha * l_s[hd] + jnp.sum(p, axis=0, keepdims=True)
            acc_s[hd] = alpha * acc_s[hd] + _dot(values_t(hd, j), p.astype(BF16))
            m_s[hd] = m_next
        return c

    lax.fori_loop(0, i, kv_body, 0)

    heads_t = []
    for hd in range(HEADS):
        o = acc_s[hd] / l_s[hd]
        o = o[:, :TQ] - lam * o[:, TQ:]
        o = o * lax.rsqrt(jnp.mean(o * o, axis=0, keepdims=True) + EPS) * subln_ref[...]
        heads_t.append(o * (1.0 - lam_init))
    ys_a_t = (jnp.concatenate(heads_t, axis=0) * _silu(gate_a_t)).astype(BF16)

    def gated(n, br):
        cols = slice(n * D_MODEL, (n + 1) * D_MODEL)
        gate = _sigmoid(_dot(hb, w_merge_ref[:, cols]) + b_merge_ref[:, cols])
        return gate * br

    def branch_out(n, y, gate_off):
        ys = (y * _silu(proj(gate_off))).astype(BF16)
        return gated(n, _dot(ys, w_branch_ref[n]))

    merged = gated(0, _dot_tn(ys_a_t, w_branch_ref[0]))

    @pl.when(i == 0)
    def _():
        ext_s[0:HALO, :] = jnp.zeros((HALO, WIDTH), F32)

    ext_s[HALO:HALO + TQ, :] = proj(OFF_P)
    pos = lax.broadcasted_iota(jnp.int32, (TQ, LANES), 0) + (i * TQ + 1)
    mixed = []
    for g, win in enumerate(POOL_WINDOWS):
        cols = slice(g * GROUP_DIM, (g + 1) * GROUP_DIM)
        u = ext_s[HALO:HALO + TQ, cols]
        wsum = u
        for d in range(1, win):
            wsum = wsum + ext_s[HALO - d:HALO - d + TQ, cols]
        count = jnp.minimum(pos, win).astype(F32)
        pooled = (wsum / count - u).astype(BF16)
        mixed.append(_dot(pooled, pool_w_ref[g]))
    y_b = (jnp.concatenate(mixed, axis=1) + pool_b_ref[...]) * pool_scale_ref[...]
    ext_s[0:HALO, :] = ext_s[TQ:TQ + HALO, :]
    merged = merged + branch_out(1, y_b, OFF_GB)

    su = _gelu(proj(OFF_SU))
    sv = _gelu(proj(OFF_SV))
    mu = jnp.mean(sv, axis=-1, keepdims=True)
    cen = sv - mu
    var = jnp.mean(cen * cen, axis=-1, keepdims=True)
    svn = (cen * lax.rsqrt(var + EPS) * ln_g_ref[...] + ln_b_ref[...]).astype(BF16)
    pr = lax.shift_right_logical(lax.broadcasted_iota(jnp.int32, (SGU_BLOCK, SGU_BLOCK), 0), 6)
    pc = lax.shift_right_logical(lax.broadcasted_iota(jnp.int32, (SGU_BLOCK, SGU_BLOCK), 1), 6)
    sgu_mask = pc <= pr
    blocks = []
    for nb in range(TQ // SGU_BLOCK):
        rws = slice(nb * SGU_BLOCK, (nb + 1) * SGU_BLOCK)
        groups = []
        for g in range(GROUPS):
            cols = slice(g * GROUP_DIM, (g + 1) * GROUP_DIM)
            wg = jnp.where(sgu_mask, sgu_w_ref[g], 0.0).astype(BF16)
            bias = jnp.broadcast_to(sgu_bt_ref[:, g:g + 1], (SGU_BLOCK, GROUP_DIM))
            groups.append(_dot(wg, svn[rws, cols]) + bias)
        blocks.append(jnp.concatenate(groups, axis=1))
    y_c = su * jnp.concatenate(blocks, axis=0)
    merged = merged + branch_out(2, y_c, OFF_GC)

    out = _dot(merged.astype(BF16), w_out_ref[...])
    out = out * lax.rsqrt(jnp.mean(out * out, axis=-1, keepdims=True) + EPS) * post_g_ref[...]
    out_ref[0] = x + out


def _full(shape):
    return pl.BlockSpec(shape, lambda b, i: (0,) * len(shape))


def _layer(x, layer_idx, pre_g, post_g, w_in, lq1, lk1, lq2, lk2, subln_g, pool_w, pool_b,
           pool_scale, ln_g, ln_b, sgu_w, sgu_b, w_branch, w_merge, b_merge, w_out):
    batch, seq, d = x.shape
    assert d == D_MODEL and seq == SEQ and seq % TQ == 0 and TQ % SGU_BLOCK == 0
    lam_init = 0.8 - 0.6 * math.exp(-0.3 * layer_idx)
    row = lambda a: a.reshape(1, -1)
    w_in_b = w_in.astype(BF16)
    w_vgt = w_in_b[:, OFF_V:OFF_V + 2 * WIDTH].T
    operands = (
        x, row(pre_g), row(post_g), w_in_b, w_vgt, row(lq1), row(lk1), row(lq2), row(lk2),
        subln_g.reshape(-1, 1), pool_w.astype(BF16), row(pool_b), row(pool_scale), row(ln_g),
        row(ln_b), sgu_w, sgu_b.T, w_branch.astype(BF16), w_merge.astype(BF16), row(b_merge),
        w_out.astype(BF16))
    x_spec = pl.BlockSpec((1, TQ, D_MODEL), lambda b, i: (b, i, 0))
    in_specs = [x_spec] + [_full(a.shape) for a in operands[1:]]
    scratch = [
        pltpu.VMEM((HEADS, 2 * TQ, LANES), BF16),
        pltpu.VMEM((HEADS, SEQ, LANES), BF16),
        pltpu.VMEM((SEQ // TQ, WIDTH, TQ), BF16),
        pltpu.VMEM((HEADS, 1, 2 * TQ), F32),
        pltpu.VMEM((HEADS, 1, 2 * TQ), F32),
        pltpu.VMEM((HEADS, V_DIM, 2 * TQ), F32),
        pltpu.VMEM((TQ + HALO, WIDTH), F32),
    ]
    return pl.pallas_call(
        functools.partial(_layer_kernel, lam_init=lam_init),
        grid=(batch, seq // TQ),
        in_specs=in_specs,
        out_specs=x_spec,
        out_shape=jax.ShapeDtypeStruct(x.shape, x.dtype),
        scratch_shapes=scratch,
        compiler_params=pltpu.CompilerParams(
            dimension_semantics=("arbitrary", "arbitrary"),
            vmem_limit_bytes=VMEM_LIMIT_BYTES),
        name=f"hybrid_layer_{layer_idx}",
    )(*operands)


def kernel(x, pre_norm_g, post_norm_g, w_in, lambda_q1, lambda_k1, lambda_q2, lambda_k2, attn_subln_g, pool_w, pool_b, pool_scale, sgu_ln_g, sgu_ln_b, sgu_w, sgu_b, w_branch, w_merge, b_merge, w_out):
    for l in range(DEPTH):
        x = _layer(x, l, pre_norm_g[l], post_norm_g[l], w_in[l], lambda_q1[l], lambda_k1[l],
                   lambda_q2[l], lambda_k2[l], attn_subln_g[l], pool_w[l], pool_b[l],
                   pool_scale[l], sgu_ln_g[l], sgu_ln_b[l], sgu_w[l], sgu_b[l], w_branch[l],
                   w_merge[l], b_merge[l], w_out[l])
    return x
```

```python
import functools
import math

import jax
import jax.numpy as jnp
from jax import lax
from jax.experimental import pallas as pl
from jax.experimental.pallas import tpu as pltpu

D_MODEL = 1024
SEQ = 2048
DEPTH = 2
CHUNK = 64
WIDTH = D_MODEL // 2
HEADS = 4
QK_DIM = 64
V_DIM = 128
POOL_WINDOWS = (2, 4, 8, 16)
GROUPS = 4
GROUP_DIM = WIDTH // GROUPS
SGU_BLOCK = 128
N_BRANCH = 3
EPS = 1e-6
HALO = 16

LANES = 128
TQ = 512
VMEM_LIMIT_BYTES = 56 * 1024 * 1024

OFF_Q, OFF_K, OFF_V, OFF_GA, OFF_P, OFF_GB, OFF_SU, OFF_SV, OFF_GC = (
    0, 512, 1024, 1536, 2048, 2560, 3072, 3584, 4096)

F32 = jnp.float32
BF16 = jnp.bfloat16


def _dot(a, b):
    return jnp.dot(a, b, preferred_element_type=F32)


def _dot_nt(a, b):
    return lax.dot_general(a, b, (((1,), (1,)), ((), ())), preferred_element_type=F32)


def _dot_tn(a, b):
    return lax.dot_general(a, b, (((0,), (0,)), ((), ())), preferred_element_type=F32)


def _silu(x):
    return x / (1.0 + jnp.exp(-x))


def _sigmoid(x):
    return 1.0 / (1.0 + jnp.exp(-x))


def _gelu(x):
    return 0.5 * x * (1.0 + lax.erf(x * math.sqrt(0.5)))


def _layer_kernel(x_ref, pre_g_ref, post_g_ref, w_in_ref, w_vgt_ref, lq1_ref, lk1_ref, lq2_ref,
                  lk2_ref, subln_ref, pool_w_ref, pool_b_ref, pool_scale_ref, ln_g_ref, ln_b_ref,
                  sgu_w_ref, sgu_bt_ref, w_branch_ref, w_merge_ref, b_merge_ref, w_out_ref,
                  out_ref,
                  q_s, k_s, vt_s, m_s, l_s, acc_s, ext_s, *, lam_init):
    i = pl.program_id(1)
    x = x_ref[0]

    h = x * lax.rsqrt(jnp.mean(x * x, axis=-1, keepdims=True) + EPS) * pre_g_ref[...]
    hb = h.astype(BF16)

    def proj(off, width=WIDTH):
        return _dot(hb, w_in_ref[:, off:off + width])

    lam = (jnp.exp(jnp.sum(lq1_ref[...] * lk1_ref[...], axis=-1, keepdims=True))
           - jnp.exp(jnp.sum(lq2_ref[...] * lk2_ref[...], axis=-1, keepdims=True))
           + lam_init)

    rows_i = pl.ds(pl.multiple_of(i * TQ, TQ), TQ)
    q = (proj(OFF_Q) * (QK_DIM ** -0.5)).astype(BF16)
    k = proj(OFF_K).astype(BF16)
    vgt = _dot_nt(w_vgt_ref[...], hb)
    vt_s[i] = vgt[:WIDTH].astype(BF16)
    gate_a_t = vgt[WIDTH:]
    first_map = lax.broadcasted_iota(jnp.int32, (TQ, LANES), 1) < QK_DIM
    zero = jnp.zeros((TQ, LANES), BF16)
    for hd in range(HEADS):
        cols = slice(hd * LANES, (hd + 1) * LANES)
        q_s[hd, 0:TQ, :] = jnp.where(first_map, q[:, cols], zero)
        q_s[hd, TQ:2 * TQ, :] = jnp.where(first_map, zero, q[:, cols])
        k_s[hd, rows_i, :] = k[:, cols]

    def scores_t(hd, j):
        rows = pl.ds(pl.multiple_of(j * TQ, TQ), TQ)
        return _dot_nt(k_s[hd, rows, :], q_s[hd])

    def values_t(hd, j):
        return vt_s[j, hd * V_DIM:(hd + 1) * V_DIM, :]

    key_chunk = lax.shift_right_logical(lax.broadcasted_iota(jnp.int32, (TQ, 2 * TQ), 0), 6)
    qry_chunk = lax.shift_right_logical(
        lax.broadcasted_iota(jnp.int32, (TQ, 2 * TQ), 1) & (TQ - 1), 6)
    visible = key_chunk <= qry_chunk
    for hd in range(HEADS):
        s = jnp.where(visible, scores_t(hd, i), -jnp.inf)
        mx = jnp.max(s, axis=0, keepdims=True)
        p = jnp.exp(s - mx)
        m_s[hd] = mx
        l_s[hd] = jnp.sum(p, axis=0, keepdims=True)
        acc_s[hd] = _dot(values_t(hd, i), p.astype(BF16))

    def kv_body(j, c):
        all_scores = [scores_t(hd, j) for hd in range(HEADS)]
        for hd in range(HEADS):
            s = all_scores[hd]
            m_prev = m_s[hd]
            m_next = jnp.maximum(m_prev, jnp.max(s, axis=0, keepdims=True))
            alpha = jnp.exp(m_prev - m_next)
            p = jnp.exp(s - m_next)
            l_s[hd] = alpha * l_s[hd] + jnp.sum(p, axis=0, keepdims=True)
            acc_s[hd] = alpha * acc_s[hd] + _dot(values_t(hd, j), p.astype(BF16))
            m_s[hd] = m_next
        return c

    lax.fori_loop(0, i, kv_body, 0)

    heads_t = []
    for hd in range(HEADS):
        o = acc_s[hd] / l_s[hd]
        o = o[:, :TQ] - lam * o[:, TQ:]
        o = o * lax.rsqrt(jnp.mean(o * o, axis=0, keepdims=True) + EPS) * subln_ref[...]
        heads_t.append(o * (1.0 - lam_init))
    ys_a_t = (jnp.concatenate(heads_t, axis=0) * _silu(gate_a_t)).astype(BF16)

    def gated(n, br):
        cols = slice(n * D_MODEL, (n + 1) * D_MODEL)
        gate = _sigmoid(_dot(hb, w_merge_ref[:, cols]) + b_merge_ref[:, cols])
        return gate * br

    def branch_out(n, y, gate_off):
        ys = (y * _silu(proj(gate_off))).astype(BF16)
        return gated(n, _dot(ys, w_branch_ref[n]))

    merged = gated(0, _dot_tn(ys_a_t, w_branch_ref[0]))

    @pl.when(i == 0)
    def _():
        ext_s[0:HALO, :] = jnp.zeros((HALO, WIDTH), F32)

    ext_s[HALO:HALO + TQ, :] = proj(OFF_P)
    pos = lax.broadcasted_iota(jnp.int32, (TQ, LANES), 0) + (i * TQ + 1)
    mixed = []
    for g, win in enumerate(POOL_WINDOWS):
        cols = slice(g * GROUP_DIM, (g + 1) * GROUP_DIM)
        u = ext_s[HALO:HALO + TQ, cols]
        wsum = u
        for d in range(1, win):
            wsum = wsum + ext_s[HALO - d:HALO - d + TQ, cols]
        count = jnp.minimum(pos, win).astype(F32)
        pooled = (wsum / count - u).astype(BF16)
        mixed.append(_dot(pooled, pool_w_ref[g]))
    y_b = (jnp.concatenate(mixed, axis=1) + pool_b_ref[...]) * pool_scale_ref[...]
    ext_s[0:HALO, :] = ext_s[TQ:TQ + HALO, :]
    merged = merged + branch_out(1, y_b, OFF_GB)

    su = _gelu(proj(OFF_SU))
    sv = _gelu(proj(OFF_SV))
    mu = jnp.mean(sv, axis=-1, keepdims=True)
    cen = sv - mu
    var = jnp.mean(cen * cen, axis=-1, keepdims=True)
    svn = (cen * lax.rsqrt(var + EPS) * ln_g_ref[...] + ln_b_ref[...]).astype(BF16)
    pr = lax.shift_right_logical(lax.broadcasted_iota(jnp.int32, (SGU_BLOCK, SGU_BLOCK), 0), 6)
    pc = lax.shift_right_logical(lax.broadcasted_iota(jnp.int32, (SGU_BLOCK, SGU_BLOCK), 1), 6)
    sgu_mask = pc <= pr
    blocks = []
    for nb in range(TQ // SGU_BLOCK):
        rws = slice(nb * SGU_BLOCK, (nb + 1) * SGU_BLOCK)
        groups = []
        for g in range(GROUPS):
            cols = slice(g * GROUP_DIM, (g + 1) * GROUP_DIM)
            wg = jnp.where(sgu_mask, sgu_w_ref[g], 0.0).astype(BF16)
            bias = jnp.broadcast_to(sgu_bt_ref[:, g:g + 1], (SGU_BLOCK, GROUP_DIM))
            groups.append(_dot(wg, svn[rws, cols]) + bias)
        blocks.append(jnp.concatenate(groups, axis=1))
    y_c = su * jnp.concatenate(blocks, axis=0)
    merged = merged + branch_out(2, y_c, OFF_GC)

    out = _dot(merged.astype(BF16), w_out_ref[...])
    out = out * lax.rsqrt(jnp.mean(out * out, axis=-1, keepdims=True) + EPS) * post_g_ref[...]
    out_ref[0] = x + out


def _full(shape):
    return pl.BlockSpec(shape, lambda b, i: (0,) * len(shape))


def _layer(x, layer_idx, pre_g, post_g, w_in, lq1, lk1, lq2, lk2, subln_g, pool_w, pool_b,
           pool_scale, ln_g, ln_b, sgu_w, sgu_b, w_branch, w_merge, b_merge, w_out):
    batch, seq, d = x.shape
    assert d == D_MODEL and seq == SEQ and seq % TQ == 0 and TQ % SGU_BLOCK == 0
    lam_init = 0.8 - 0.6 * math.exp(-0.3 * layer_idx)
    row = lambda a: a.reshape(1, -1)
    w_in_b = w_in.astype(BF16)
    w_vgt = w_in_b[:, OFF_V:OFF_V + 2 * WIDTH].T
    operands = (
        x, row(pre_g), row(post_g), w_in_b, w_vgt, row(lq1), row(lk1), row(lq2), row(lk2),
        subln_g.reshape(-1, 1), pool_w.astype(BF16), row(pool_b), row(pool_scale), row(ln_g),
        row(ln_b), sgu_w, sgu_b.T, w_branch.astype(BF16), w_merge.astype(BF16), row(b_merge),
        w_out.astype(BF16))
    x_spec = pl.BlockSpec((1, TQ, D_MODEL), lambda b, i: (b, i, 0))
    in_specs = [x_spec] + [_full(a.shape) for a in operands[1:]]
    scratch = [
        pltpu.VMEM((HEADS, 2 * TQ, LANES), BF16),
        pltpu.VMEM((HEADS, SEQ, LANES), BF16),
        pltpu.VMEM((SEQ // TQ, WIDTH, TQ), BF16),
        pltpu.VMEM((HEADS, 1, 2 * TQ), F32),
        pltpu.VMEM((HEADS, 1, 2 * TQ), F32),
        pltpu.VMEM((HEADS, V_DIM, 2 * TQ), F32),
        pltpu.VMEM((TQ + HALO, WIDTH), F32),
    ]
    return pl.pallas_call(
        functools.partial(_layer_kernel, lam_init=lam_init),
        grid=(batch, seq // TQ),
        in_specs=in_specs,
        out_specs=x_spec,
        out_shape=jax.ShapeDtypeStruct(x.shape, x.dtype),
        scratch_shapes=scratch,
        compiler_params=pltpu.CompilerParams(
            dimension_semantics=("arbitrary", "arbitrary"),
            vmem_limit_bytes=VMEM_LIMIT_BYTES),
        name=f"hybrid_layer_{layer_idx}",
    )(*operands)


def kernel(x, pre_norm_g, post_norm_g, w_in, lambda_q1, lambda_k1, lambda_q2, lambda_k2, attn_subln_g, pool_w, pool_b, pool_scale, sgu_ln_g, sgu_ln_b, sgu_w, sgu_b, w_branch, w_merge, b_merge, w_out):
    for l in range(DEPTH):
        x = _layer(x, l, pre_norm_g[l], post_norm_g[l], w_in[l], lambda_q1[l], lambda_k1[l],
                   lambda_q2[l], lambda_k2[l], attn_subln_g[l], pool_w[l], pool_b[l],
                   pool_scale[l], sgu_ln_g[l], sgu_ln_b[l], sgu_w[l], sgu_b[l], w_branch[l],
                   w_merge[l], b_merge[l], w_out[l])
    return x
```

```python
import functools
import math

import jax
import jax.numpy as jnp
from jax import lax
from jax.experimental import pallas as pl
from jax.experimental.pallas import tpu as pltpu

D_MODEL = 1024
SEQ = 2048
DEPTH = 2
CHUNK = 64
WIDTH = D_MODEL // 2
HEADS = 4
QK_DIM = 64
V_DIM = 128
V_EXT = V_DIM + 16
POOL_WINDOWS = (2, 4, 8, 16)
GROUPS = 4
GROUP_DIM = WIDTH // GROUPS
SGU_BLOCK = 128
N_BRANCH = 3
EPS = 1e-6
HALO = 16

LANES = 128
TQ = 512
VMEM_LIMIT_BYTES = 60 * 1024 * 1024

IN_OFF_V = 1024
OFF_Q, OFF_K, OFF_P, OFF_GB, OFF_SU, OFF_SV, OFF_GC = (0, 512, 1024, 1536, 2048, 2560, 3072)

F32 = jnp.float32
BF16 = jnp.bfloat16


def _dot(a, b):
    return jnp.dot(a, b, preferred_element_type=F32)


def _dot_nt(a, b):
    return lax.dot_general(a, b, (((1,), (1,)), ((), ())), preferred_element_type=F32)


def _dot_tn(a, b):
    return lax.dot_general(a, b, (((0,), (0,)), ((), ())), preferred_element_type=F32)


def _silu(x):
    return x / (1.0 + jnp.exp(-x))


def _sigmoid(x):
    return 1.0 / (1.0 + jnp.exp(-x))


def _gelu(x):
    return 0.5 * x * (1.0 + lax.erf(x * math.sqrt(0.5)))


def _layer_kernel(x_ref, pre_g_ref, post_g_ref, w_in_ref, w_vgt_ref, lq1_ref, lk1_ref, lq2_ref,
                  lk2_ref, subln_ref, pool_w_ref, pool_b_ref, pool_scale_ref, ln_g_ref, ln_b_ref,
                  sgu_w_ref, sgu_bt_ref, w_branch_ref, w_merge_ref, b_merge_ref, w_out_ref,
                  out_ref,
                  q_s, k_s, vt_s, m_s, acc_s, ext_s, merged_s, *, lam_init):
    i = pl.program_id(1)
    x = x_ref[0]

    h = x * lax.rsqrt(jnp.mean(x * x, axis=-1, keepdims=True) + EPS) * pre_g_ref[...]
    hb = h.astype(BF16)

    def proj(off, width=WIDTH):
        return _dot(hb, w_in_ref[:, off:off + width])

    def merge_gate(n):
        cols = slice(n * D_MODEL, (n + 1) * D_MODEL)
        return _sigmoid(_dot(hb, w_merge_ref[:, cols]) + b_merge_ref[:, cols])

    lam = (jnp.exp(jnp.sum(lq1_ref[...] * lk1_ref[...], axis=-1, keepdims=True))
           - jnp.exp(jnp.sum(lq2_ref[...] * lk2_ref[...], axis=-1, keepdims=True))
           + lam_init)

    rows_i = pl.ds(pl.multiple_of(i * TQ, TQ), TQ)
    q = (proj(OFF_Q) * (QK_DIM ** -0.5 * math.log2(math.e))).astype(BF16)
    k = proj(OFF_K).astype(BF16)
    vgt = _dot_nt(w_vgt_ref[...], hb)
    gate_a_t = vgt[WIDTH:]
    ones = jnp.ones((V_EXT - V_DIM, TQ), BF16)
    for hd in range(HEADS):
        vt_s[i, hd, 0:V_DIM, :] = vgt[hd * V_DIM:(hd + 1) * V_DIM].astype(BF16)
        vt_s[i, hd, V_DIM:V_EXT, :] = ones
    first_map = lax.broadcasted_iota(jnp.int32, (TQ, LANES), 1) < QK_DIM
    zero = jnp.zeros((TQ, LANES), BF16)
    for hd in range(HEADS):
        cols = slice(hd * LANES, (hd + 1) * LANES)
        q_s[hd, 0:TQ, :] = jnp.where(first_map, q[:, cols], zero)
        q_s[hd, TQ:2 * TQ, :] = jnp.where(first_map, zero, q[:, cols])
        k_s[hd, rows_i, :] = k[:, cols]

    def scores_t(hd, j):
        rows = pl.ds(pl.multiple_of(j * TQ, TQ), TQ)
        return _dot_nt(k_s[hd, rows, :], q_s[hd])

    def values_t(hd, j):
        return vt_s[j, hd]

    key_chunk = lax.shift_right_logical(lax.broadcasted_iota(jnp.int32, (TQ, 2 * TQ), 0), 6)
    qry_chunk = lax.shift_right_logical(
        lax.broadcasted_iota(jnp.int32, (TQ, 2 * TQ), 1) & (TQ - 1), 6)
    visible = key_chunk <= qry_chunk

    def diag_probs(hd):
        s = jnp.where(visible, scores_t(hd, i), -jnp.inf)
        mx = jnp.max(s, axis=0, keepdims=True)
        m_s[hd] = mx
        return jnp.exp2(s - mx).astype(BF16)

    def diag_values(hd, p):
        acc_s[hd] = _dot(values_t(hd, i), p)

    p0 = diag_probs(0)

    su = _gelu(proj(OFF_SU))
    p1 = diag_probs(1)
    sv = _gelu(proj(OFF_SV))
    diag_values(0, p0)
    mu = jnp.mean(sv, axis=-1, keepdims=True)
    cen = sv - mu
    var = jnp.mean(cen * cen, axis=-1, keepdims=True)
    svn = (cen * lax.rsqrt(var + EPS) * ln_g_ref[...] + ln_b_ref[...]).astype(BF16)
    p2 = diag_probs(2)

    @pl.when(i == 0)
    def _():
        ext_s[0:HALO, :] = jnp.zeros((HALO, WIDTH), F32)

    ext_s[HALO:HALO + TQ, :] = proj(OFF_P)
    gate_b = _silu(proj(OFF_GB))
    diag_values(1, p1)
    p3 = diag_probs(3)
    gate_c = _silu(proj(OFF_GC))
    diag_values(2, p2)
    merge_b = merge_gate(1)
    diag_values(3, p3)
    merge_c = merge_gate(2)
    pos = lax.broadcasted_iota(jnp.int32, (TQ, LANES), 0) + (i * TQ + 1)
    mixed = []
    for g, win in enumerate(POOL_WINDOWS):
        cols = slice(g * GROUP_DIM, (g + 1) * GROUP_DIM)
        u = ext_s[HALO:HALO + TQ, cols]
        wsum = u
        for d in range(1, win):
            wsum = wsum + ext_s[HALO - d:HALO - d + TQ, cols]
        count = jnp.minimum(pos, win).astype(F32)
        pooled = (wsum / count - u).astype(BF16)
        mixed.append(_dot(pooled, pool_w_ref[g]))
    y_b = (jnp.concatenate(mixed, axis=1) + pool_b_ref[...]) * pool_scale_ref[...]
    ext_s[0:HALO, :] = ext_s[TQ:TQ + HALO, :]

    pr = lax.shift_right_logical(lax.broadcasted_iota(jnp.int32, (SGU_BLOCK, SGU_BLOCK), 0), 6)
    pc = lax.shift_right_logical(lax.broadcasted_iota(jnp.int32, (SGU_BLOCK, SGU_BLOCK), 1), 6)
    sgu_mask = pc <= pr
    blocks = []
    for nb in range(TQ // SGU_BLOCK):
        rws = slice(nb * SGU_BLOCK, (nb + 1) * SGU_BLOCK)
        groups = []
        for g in range(GROUPS):
            cols = slice(g * GROUP_DIM, (g + 1) * GROUP_DIM)
            wg = jnp.where(sgu_mask, sgu_w_ref[g], 0.0).astype(BF16)
            bias = jnp.broadcast_to(sgu_bt_ref[:, g:g + 1], (SGU_BLOCK, GROUP_DIM))
            groups.append(_dot(wg, svn[rws, cols]) + bias)
        blocks.append(jnp.concatenate(groups, axis=1))
    y_c = su * jnp.concatenate(blocks, axis=0)

    branch_b = _dot((y_b * gate_b).astype(BF16), w_branch_ref[1])
    branch_c = _dot((y_c * gate_c).astype(BF16), w_branch_ref[2])
    merged_s[...] = merge_b * branch_b + merge_c * branch_c

    def kv_body(j, c):
        all_scores = [scores_t(hd, j) for hd in range(HEADS)]
        for hd in range(HEADS):
            s = all_scores[hd]
            m_prev = m_s[hd]
            m_next = jnp.maximum(m_prev, jnp.max(s, axis=0, keepdims=True))
            alpha = jnp.exp2(m_prev - m_next)
            p = jnp.exp2(s - m_next).astype(BF16)
            acc_s[hd] = alpha * acc_s[hd] + _dot(values_t(hd, j), p)
            m_s[hd] = m_next
        return c

    lax.fori_loop(0, i, kv_body, 0)

    merge_a = merge_gate(0)
    heads_t = []
    for hd in range(HEADS):
        o = acc_s[hd, 0:V_DIM, :] / acc_s[hd, V_DIM:V_DIM + 1, :]
        o = o[:, :TQ] - lam * o[:, TQ:]
        o = o * lax.rsqrt(jnp.mean(o * o, axis=0, keepdims=True) + EPS) * subln_ref[...]
        heads_t.append(o * (1.0 - lam_init))
    ys_a_t = (jnp.concatenate(heads_t, axis=0) * _silu(gate_a_t)).astype(BF16)
    merged = merge_a * _dot_tn(ys_a_t, w_branch_ref[0]) + merged_s[...]

    out = _dot(merged.astype(BF16), w_out_ref[...])
    out = out * lax.rsqrt(jnp.mean(out * out, axis=-1, keepdims=True) + EPS) * post_g_ref[...]
    out_ref[0] = x + out


def _full(shape):
    return pl.BlockSpec(shape, lambda b, i: (0,) * len(shape))


def _layer(x, layer_idx, pre_g, post_g, w_in, lq1, lk1, lq2, lk2, subln_g, pool_w, pool_b,
           pool_scale, ln_g, ln_b, sgu_w, sgu_b, w_branch, w_merge, b_merge, w_out):
    batch, seq, d = x.shape
    assert d == D_MODEL and seq == SEQ and seq % TQ == 0 and TQ % SGU_BLOCK == 0
    lam_init = 0.8 - 0.6 * math.exp(-0.3 * layer_idx)
    row = lambda a: a.reshape(1, -1)
    w_in_b = w_in.astype(BF16)
    w_vgt = w_in_b[:, IN_OFF_V:IN_OFF_V + 2 * WIDTH].T
    w_rest = jnp.concatenate([w_in_b[:, :IN_OFF_V], w_in_b[:, IN_OFF_V + 2 * WIDTH:]], axis=1)
    operands = (
        x, row(pre_g), row(post_g), w_rest, w_vgt, row(lq1), row(lk1), row(lq2), row(lk2),
        subln_g.reshape(-1, 1), pool_w.astype(BF16), row(pool_b), row(pool_scale), row(ln_g),
        row(ln_b), sgu_w, sgu_b.T, w_branch.astype(BF16), w_merge.astype(BF16), row(b_merge),
        w_out.astype(BF16))
    x_spec = pl.BlockSpec((1, TQ, D_MODEL), lambda b, i: (b, i, 0))
    in_specs = [x_spec] + [_full(a.shape) for a in operands[1:]]
    scratch = [
        pltpu.VMEM((HEADS, 2 * TQ, LANES), BF16),
        pltpu.VMEM((HEADS, SEQ, LANES), BF16),
        pltpu.VMEM((SEQ // TQ, HEADS, V_EXT, TQ), BF16),
        pltpu.VMEM((HEADS, 1, 2 * TQ), F32),
        pltpu.VMEM((HEADS, V_EXT, 2 * TQ), F32),
        pltpu.VMEM((TQ + HALO, WIDTH), F32),
        pltpu.VMEM((TQ, D_MODEL), F32),
    ]
    return pl.pallas_call(
        functools.partial(_layer_kernel, lam_init=lam_init),
        grid=(batch, seq // TQ),
        in_specs=in_specs,
        out_specs=x_spec,
        out_shape=jax.ShapeDtypeStruct(x.shape, x.dtype),
        scratch_shapes=scratch,
        compiler_params=pltpu.CompilerParams(
            dimension_semantics=("arbitrary", "arbitrary"),
            vmem_limit_bytes=VMEM_LIMIT_BYTES),
        name=f"hybrid_layer_{layer_idx}",
    )(*operands)


def kernel(x, pre_norm_g, post_norm_g, w_in, lambda_q1, lambda_k1, lambda_q2, lambda_k2, attn_subln_g, pool_w, pool_b, pool_scale, sgu_ln_g, sgu_ln_b, sgu_w, sgu_b, w_branch, w_merge, b_merge, w_out):
    for l in range(DEPTH):
        x = _layer(x, l, pre_norm_g[l], post_norm_g[l], w_in[l], lambda_q1[l], lambda_k1[l],
                   lambda_q2[l], lambda_k2[l], attn_subln_g[l], pool_w[l], pool_b[l],
                   pool_scale[l], sgu_ln_g[l], sgu_ln_b[l], sgu_w[l], sgu_b[l], w_branch[l],
                   w_merge[l], b_merge[l], w_out[l])
    return x
```

```python
import functools
import math

import jax
import jax.numpy as jnp
from jax import lax
from jax.experimental import pallas as pl
from jax.experimental.pallas import tpu as pltpu

D_MODEL = 1024
SEQ = 2048
DEPTH = 2
CHUNK = 64
WIDTH = D_MODEL // 2
HEADS = 4
QK_DIM = 64
V_DIM = 128
V_EXT = V_DIM + 16
POOL_WINDOWS = (2, 4, 8, 16)
GROUPS = 4
GROUP_DIM = WIDTH // GROUPS
SGU_BLOCK = 128
N_BRANCH = 3
EPS = 1e-6
LANES = 128
SUBLANES = 8
HALO = SUBLANES * (max(POOL_WINDOWS).bit_length() - 1)
TQ = 512
COL_STRIP = 256
AHEAD = 8
VMEM_LIMIT_BYTES = 60 * 1024 * 1024

OFF_Q, OFF_K, OFF_V, OFF_GA, OFF_P, OFF_GB, OFF_SU, OFF_SV, OFF_GC = (
    0, 512, 1024, 1536, 2048, 2560, 3072, 3584, 4096)

F32 = jnp.float32
BF16 = jnp.bfloat16


def _dot(a, b):
    return jnp.dot(a, b, preferred_element_type=F32)


def _silu_of_half(xh):
    return xh * (1.0 + jnp.tanh(xh))


def _twice_sigmoid_of_half(xh):
    return 1.0 + jnp.tanh(xh)


def _gelu(x):
    return 0.5 * x * (1.0 + lax.erf(x * math.sqrt(0.5)))


def _layer_kernel(x_ref, pre_g_ref, post_g_ref, w_in_ref, lq1_ref, lk1_ref, lq2_ref,
                  lk2_ref, subln_ref, pool_w_ref, pool_b_ref, pool_scale_ref, ln_g_ref, ln_b_ref,
                  sgu_w_ref, sgu_bt_ref, w_branch_ref, w_merge_ref, b_merge_ref, w_out_ref,
                  out_ref,
                  qt_s, k_s, vt_s, m_s, acc_s, ext_s, merged_s, *, lam_init):
    i = pl.program_id(1)

    @pl.when(i == 0)
    def _():
        ext_s[0:HALO, :] = jnp.zeros((HALO, WIDTH), F32)

    x = x_ref[0]

    h = x * lax.rsqrt(jnp.mean(x * x, axis=-1, keepdims=True) + EPS) * pre_g_ref[...]
    hb = h.astype(BF16)

    def proj(off, width=WIDTH):
        return _dot(hb, w_in_ref[:, off:off + width])

    def merge_gate(n):
        cols = slice(n * D_MODEL, (n + 1) * D_MODEL)
        return _twice_sigmoid_of_half(_dot(hb, w_merge_ref[:, cols]) + b_merge_ref[:, cols])

    lam = (jnp.exp(jnp.sum(lq1_ref[...] * lk1_ref[...], axis=-1, keepdims=True))
           - jnp.exp(jnp.sum(lq2_ref[...] * lk2_ref[...], axis=-1, keepdims=True))
           + lam_init)

    rows_i = pl.ds(pl.multiple_of(i * TQ, TQ), TQ)
    q_t = (proj(OFF_Q) * (QK_DIM ** -0.5 * math.log2(math.e))).T
    k = proj(OFF_K).astype(BF16)
    v_t = proj(OFF_V).T
    ones = jnp.ones((V_EXT - V_DIM, TQ), BF16)
    for hd in range(HEADS):
        vt_s[i, hd, 0:V_DIM, :] = v_t[hd * V_DIM:(hd + 1) * V_DIM].astype(BF16)
        vt_s[i, hd, V_DIM:V_EXT, :] = ones
    first_map = lax.broadcasted_iota(jnp.int32, (LANES, TQ), 0) < QK_DIM
    zero = jnp.zeros((LANES, TQ), BF16)
    for hd in range(HEADS):
        cols = slice(hd * LANES, (hd + 1) * LANES)
        q_hd = q_t[cols].astype(BF16)
        qt_s[hd, :, 0:TQ] = jnp.where(first_map, q_hd, zero)
        qt_s[hd, :, TQ:2 * TQ] = jnp.where(first_map, zero, q_hd)
        k_s[hd, rows_i, :] = k[:, cols]

    def values_t(hd, j):
        return vt_s[j, hd]

    col_strips = [slice(c, c + COL_STRIP) for c in range(0, 2 * TQ, COL_STRIP)]

    def visible_keys(cs):
        return cs.start % TQ + COL_STRIP

    def diag_probs(hd):
        probs = []
        for cs in col_strips:
            nk = visible_keys(cs)
            keys = k_s[hd, pl.ds(pl.multiple_of(i * TQ, TQ), nk), :]
            s = _dot(keys, qt_s[hd, :, cs])
            key_chunk = lax.shift_right_logical(
                lax.broadcasted_iota(jnp.int32, (nk, COL_STRIP), 0), 6)
            qry_chunk = lax.shift_right_logical(
                lax.broadcasted_iota(jnp.int32, (nk, COL_STRIP), 1) + cs.start % TQ, 6)
            s = jnp.where(key_chunk <= qry_chunk, s, -jnp.inf)
            mx = jnp.max(s, axis=0, keepdims=True)
            m_s[hd, :, cs] = mx
            probs.append(jnp.exp2(s - mx).astype(BF16))
        return probs

    def diag_values(hd, probs):
        for cs, p in zip(col_strips, probs):
            acc_s[hd, :, cs] = _dot(vt_s[i, hd, :, 0:visible_keys(cs)], p)

    p0 = diag_probs(0)

    su = _gelu(proj(OFF_SU))
    p1 = diag_probs(1)
    sv = _gelu(proj(OFF_SV))
    diag_values(0, p0)
    mu = jnp.mean(sv, axis=-1, keepdims=True)
    cen = sv - mu
    var = jnp.mean(cen * cen, axis=-1, keepdims=True)
    svn = (cen * lax.rsqrt(var + EPS) * ln_g_ref[...] + ln_b_ref[...]).astype(BF16)
    p2 = diag_probs(2)

    ext_s[HALO:HALO + TQ, :] = proj(OFF_P)
    gate_b = _silu_of_half(proj(OFF_GB))
    diag_values(1, p1)
    p3 = diag_probs(3)
    pos = lax.broadcasted_iota(jnp.int32, (TQ, LANES), 0) + (i * TQ + 1)
    mixed = []
    for g, win in enumerate(POOL_WINDOWS):
        cols = slice(g * GROUP_DIM, (g + 1) * GROUP_DIM)
        wsum = ext_s[:, cols]
        shift = 1
        while shift < win:
            rows = wsum.shape[0] - SUBLANES
            wsum = wsum[SUBLANES:] + wsum[SUBLANES - shift:SUBLANES - shift + rows]
            shift *= 2
        wsum = wsum[wsum.shape[0] - TQ:]
        u = ext_s[HALO:HALO + TQ, cols]
        count = jnp.minimum(pos, win).astype(F32)
        pooled = (wsum / count - u).astype(BF16)
        mixed.append(_dot(pooled, pool_w_ref[g]))
    y_b = (jnp.concatenate(mixed, axis=1) + pool_b_ref[...]) * pool_scale_ref[...]
    ext_s[0:HALO, :] = ext_s[TQ:TQ + HALO, :]
    merge_b = merge_gate(1)
    merged_s[...] = merge_b * _dot((y_b * gate_b).astype(BF16), w_branch_ref[1])

    gate_c = _silu_of_half(proj(OFF_GC))
    diag_values(2, p2)
    pr = lax.shift_right_logical(lax.broadcasted_iota(jnp.int32, (SGU_BLOCK, SGU_BLOCK), 0), 6)
    pc = lax.shift_right_logical(lax.broadcasted_iota(jnp.int32, (SGU_BLOCK, SGU_BLOCK), 1), 6)
    sgu_mask = pc <= pr
    blocks = []
    for nb in range(TQ // SGU_BLOCK):
        rws = slice(nb * SGU_BLOCK, (nb + 1) * SGU_BLOCK)
        groups = []
        for g in range(GROUPS):
            cols = slice(g * GROUP_DIM, (g + 1) * GROUP_DIM)
            wg = jnp.where(sgu_mask, sgu_w_ref[g], 0.0).astype(BF16)
            bias = jnp.broadcast_to(sgu_bt_ref[:, g:g + 1], (SGU_BLOCK, GROUP_DIM))
            groups.append(_dot(wg, svn[rws, cols]) + bias)
        blocks.append(jnp.concatenate(groups, axis=1))
    y_c = su * jnp.concatenate(blocks, axis=0)
    merge_c = merge_gate(2)
    diag_values(3, p3)
    merged_s[...] += merge_c * _dot((y_c * gate_c).astype(BF16), w_branch_ref[2])

    strips = [(hd, cs) for hd in range(HEADS) for cs in col_strips]

    def strip_scores(hd, cs, j):
        rows = pl.ds(pl.multiple_of(j * TQ, TQ), TQ)
        return _dot(k_s[hd, rows, :], qt_s[hd, :, cs])

    def strip_update(hd, cs, j, s):
        m_prev = m_s[hd, :, cs]
        m_next = jnp.maximum(m_prev, jnp.max(s, axis=0, keepdims=True))
        alpha = jnp.exp2(m_prev - m_next)
        p = jnp.exp2(s - m_next).astype(BF16)
        acc_s[hd, :, cs] = alpha * acc_s[hd, :, cs] + _dot(values_t(hd, j), p)
        m_s[hd, :, cs] = m_next

    def kv_body(j, c):
        pending = []
        for hd, cs in strips:
            pending.append((hd, cs, strip_scores(hd, cs, j)))
            if len(pending) > AHEAD:
                strip_update(*pending[0][:2], j, pending[0][2])
                pending.pop(0)
        for hd, cs, s in pending:
            strip_update(hd, cs, j, s)
        return c

    lax.fori_loop(0, i, kv_body, 0)

    gate_a = _silu_of_half(proj(OFF_GA))
    merge_a = merge_gate(0)
    heads_t = []
    for hd in range(HEADS):
        o = acc_s[hd, 0:V_DIM, :] / acc_s[hd, V_DIM:V_DIM + 1, :]
        o = o[:, :TQ] - lam * o[:, TQ:]
        o = o * lax.rsqrt(jnp.mean(o * o, axis=0, keepdims=True) + EPS) * subln_ref[...]
        heads_t.append(o * (1.0 - lam_init))
    y_a = jnp.concatenate(heads_t, axis=0).T
    merged = merge_a * _dot((y_a * gate_a).astype(BF16), w_branch_ref[0]) + merged_s[...]

    out = _dot(merged.astype(BF16), w_out_ref[...])
    out = out * lax.rsqrt(jnp.mean(out * out, axis=-1, keepdims=True) + EPS) * post_g_ref[...]
    out_ref[0] = x + out


def _full(shape):
    return pl.BlockSpec(shape, lambda b, i: (0,) * len(shape))


def _layer(x, layer_idx, pre_g, post_g, w_in, lq1, lk1, lq2, lk2, subln_g, pool_w, pool_b,
           pool_scale, ln_g, ln_b, sgu_w, sgu_b, w_branch, w_merge, b_merge, w_out):
    batch, seq, d = x.shape
    assert d == D_MODEL and seq == SEQ and seq % TQ == 0 and TQ % SGU_BLOCK == 0
    lam_init = 0.8 - 0.6 * math.exp(-0.3 * layer_idx)
    row = lambda a: a.reshape(1, -1)
    col = jnp.arange(w_in.shape[1])
    is_gate = ((col >= OFF_GA) & (col < OFF_P)) | ((col >= OFF_GB) & (col < OFF_SU)) | (col >= OFF_GC)
    w_in = w_in * jnp.where(is_gate, 0.5, 1.0).astype(w_in.dtype)
    w_merge, b_merge, w_out = 0.5 * w_merge, 0.5 * b_merge, 0.5 * w_out
    operands = (
        x, row(pre_g), row(post_g), w_in.astype(BF16), row(lq1), row(lk1), row(lq2), row(lk2),
        subln_g.reshape(-1, 1), pool_w.astype(BF16), row(pool_b), row(pool_scale), row(ln_g),
        row(ln_b), sgu_w, sgu_b.T, w_branch.astype(BF16), w_merge.astype(BF16), row(b_merge),
        w_out.astype(BF16))
    x_spec = pl.BlockSpec((1, TQ, D_MODEL), lambda b, i: (b, i, 0))
    in_specs = [x_spec] + [_full(a.shape) for a in operands[1:]]
    scratch = [
        pltpu.VMEM((HEADS, LANES, 2 * TQ), BF16),
        pltpu.VMEM((HEADS, SEQ, LANES), BF16),
        pltpu.VMEM((SEQ // TQ, HEADS, V_EXT, TQ), BF16),
        pltpu.VMEM((HEADS, 1, 2 * TQ), F32),
        pltpu.VMEM((HEADS, V_EXT, 2 * TQ), F32),
        pltpu.VMEM((TQ + HALO, WIDTH), F32),
        pltpu.VMEM((TQ, D_MODEL), F32),
    ]
    return pl.pallas_call(
        functools.partial(_layer_kernel, lam_init=lam_init),
        grid=(batch, seq // TQ),
        in_specs=in_specs,
        out_specs=x_spec,
        out_shape=jax.ShapeDtypeStruct(x.shape, x.dtype),
        scratch_shapes=scratch,
        compiler_params=pltpu.CompilerParams(
            dimension_semantics=("arbitrary", "arbitrary"),
            vmem_limit_bytes=VMEM_LIMIT_BYTES),
        name=f"hybrid_layer_{layer_idx}",
    )(*operands)


def kernel(x, pre_norm_g, post_norm_g, w_in, lambda_q1, lambda_k1, lambda_q2, lambda_k2, attn_subln_g, pool_w, pool_b, pool_scale, sgu_ln_g, sgu_ln_b, sgu_w, sgu_b, w_branch, w_merge, b_merge, w_out):
    for l in range(DEPTH):
        x = _layer(x, l, pre_norm_g[l], post_norm_g[l], w_in[l], lambda_q1[l], lambda_k1[l],
                   lambda_q2[l], lambda_k2[l], attn_subln_g[l], pool_w[l], pool_b[l],
                   pool_scale[l], sgu_ln_g[l], sgu_ln_b[l], sgu_w[l], sgu_b[l], w_branch[l],
                   w_merge[l], b_merge[l], w_out[l])
    return x
```

```python
import functools
import math

import jax
import jax.numpy as jnp
from jax import lax
from jax.experimental import pallas as pl
from jax.experimental.pallas import tpu as pltpu

D_MODEL = 1024
SEQ = 2048
DEPTH = 2
CHUNK = 64
WIDTH = D_MODEL // 2
HEADS = 4
QK_DIM = 64
V_DIM = 128
V_EXT = V_DIM + 16
POOL_WINDOWS = (2, 4, 8, 16)
GROUPS = 4
GROUP_DIM = WIDTH // GROUPS
SGU_BLOCK = 128
N_BRANCH = 3
EPS = 1e-6
LANES = 128
SUBLANES = 8
HALO = SUBLANES * (max(POOL_WINDOWS).bit_length() - 1)
TQ = 512
COL_STRIP = 256
AHEAD = 8
VMEM_LIMIT_BYTES = 60 * 1024 * 1024

OFF_Q, OFF_K, OFF_V, OFF_GA, OFF_P, OFF_GB, OFF_SU, OFF_SV, OFF_GC = (
    0, 512, 1024, 1536, 2048, 2560, 3072, 3584, 4096)

F32 = jnp.float32
BF16 = jnp.bfloat16


def _dot(a, b):
    return jnp.dot(a, b, preferred_element_type=F32)


def _silu_of_half(xh):
    return xh * (1.0 + jnp.tanh(xh))


def _twice_sigmoid_of_half(xh):
    return 1.0 + jnp.tanh(xh)


def _gelu(x):
    return 0.5 * x * (1.0 + lax.erf(x * math.sqrt(0.5)))


def _layer_kernel(x_ref, pre_g_ref, post_g_ref, w_in_ref, lq1_ref, lk1_ref, lq2_ref,
                  lk2_ref, subln_ref, pool_w_ref, pool_b_ref, pool_scale_ref, ln_g_ref, ln_b_ref,
                  sgu_w_ref, sgu_bt_ref, w_branch_ref, w_merge_ref, b_merge_ref, w_out_ref,
                  out_ref,
                  qt_s, k_s, vt_s, m_s, acc_s, ext_s, merged_s, *, lam_init):
    refs = (x_ref, pre_g_ref, post_g_ref, w_in_ref, lq1_ref, lk1_ref, lq2_ref,
            lk2_ref, subln_ref, pool_w_ref, pool_b_ref, pool_scale_ref, ln_g_ref, ln_b_ref,
            sgu_w_ref, sgu_bt_ref, w_branch_ref, w_merge_ref, b_merge_ref, w_out_ref,
            out_ref, qt_s, k_s, vt_s, m_s, acc_s, ext_s, merged_s)
    for blk in range(SEQ // TQ):
        pl.when(pl.program_id(1) == blk)(functools.partial(_block_step, blk, lam_init, *refs))


def _block_step(i, lam_init, x_ref, pre_g_ref, post_g_ref, w_in_ref, lq1_ref, lk1_ref, lq2_ref,
                lk2_ref, subln_ref, pool_w_ref, pool_b_ref, pool_scale_ref, ln_g_ref, ln_b_ref,
                sgu_w_ref, sgu_bt_ref, w_branch_ref, w_merge_ref, b_merge_ref, w_out_ref,
                out_ref, qt_s, k_s, vt_s, m_s, acc_s, ext_s, merged_s):
    if i == 0:
        ext_s[0:HALO, :] = jnp.zeros((HALO, WIDTH), F32)

    x = x_ref[0]

    h = x * lax.rsqrt(jnp.mean(x * x, axis=-1, keepdims=True) + EPS) * pre_g_ref[...]
    hb = h.astype(BF16)

    def proj(off, width=WIDTH):
        return _dot(hb, w_in_ref[:, off:off + width])

    def merge_gate(n):
        cols = slice(n * D_MODEL, (n + 1) * D_MODEL)
        return _twice_sigmoid_of_half(_dot(hb, w_merge_ref[:, cols]) + b_merge_ref[:, cols])

    lam = (jnp.exp(jnp.sum(lq1_ref[...] * lk1_ref[...], axis=-1, keepdims=True))
           - jnp.exp(jnp.sum(lq2_ref[...] * lk2_ref[...], axis=-1, keepdims=True))
           + lam_init)

    rows_i = slice(i * TQ, (i + 1) * TQ)
    q_t = (proj(OFF_Q) * (QK_DIM ** -0.5 * math.log2(math.e))).T
    k = proj(OFF_K).astype(BF16)
    v_t = proj(OFF_V).T
    ones = jnp.ones((V_EXT - V_DIM, TQ), BF16)
    for hd in range(HEADS):
        vt_s[i, hd, 0:V_DIM, :] = v_t[hd * V_DIM:(hd + 1) * V_DIM].astype(BF16)
        vt_s[i, hd, V_DIM:V_EXT, :] = ones
    first_map = lax.broadcasted_iota(jnp.int32, (LANES, TQ), 0) < QK_DIM
    zero = jnp.zeros((LANES, TQ), BF16)
    for hd in range(HEADS):
        cols = slice(hd * LANES, (hd + 1) * LANES)
        q_hd = q_t[cols].astype(BF16)
        qt_s[hd, :, 0:TQ] = jnp.where(first_map, q_hd, zero)
        qt_s[hd, :, TQ:2 * TQ] = jnp.where(first_map, zero, q_hd)
        k_s[hd, rows_i, :] = k[:, cols]

    def values_t(hd, j):
        return vt_s[j, hd]

    col_strips = [slice(c, c + COL_STRIP) for c in range(0, 2 * TQ, COL_STRIP)]

    def visible_keys(cs):
        return cs.start % TQ + COL_STRIP

    def diag_probs(hd):
        probs = []
        for cs in col_strips:
            nk = visible_keys(cs)
            keys = k_s[hd, i * TQ:i * TQ + nk, :]
            s = _dot(keys, qt_s[hd, :, cs])
            key_chunk = lax.shift_right_logical(
                lax.broadcasted_iota(jnp.int32, (nk, COL_STRIP), 0), 6)
            qry_chunk = lax.shift_right_logical(
                lax.broadcasted_iota(jnp.int32, (nk, COL_STRIP), 1) + cs.start % TQ, 6)
            s = jnp.where(key_chunk <= qry_chunk, s, -jnp.inf)
            mx = jnp.max(s, axis=0, keepdims=True)
            m_s[hd, :, cs] = mx
            probs.append(jnp.exp2(s - mx).astype(BF16))
        return probs

    def diag_values(hd, probs):
        for cs, p in zip(col_strips, probs):
            acc_s[hd, :, cs] = _dot(vt_s[i, hd, :, 0:visible_keys(cs)], p)

    p0 = diag_probs(0)

    su = _gelu(proj(OFF_SU))
    p1 = diag_probs(1)
    sv = _gelu(proj(OFF_SV))
    diag_values(0, p0)
    mu = jnp.mean(sv, axis=-1, keepdims=True)
    cen = sv - mu
    var = jnp.mean(cen * cen, axis=-1, keepdims=True)
    svn = (cen * lax.rsqrt(var + EPS) * ln_g_ref[...] + ln_b_ref[...]).astype(BF16)
    p2 = diag_probs(2)

    ext_s[HALO:HALO + TQ, :] = proj(OFF_P)
    gate_b = _silu_of_half(proj(OFF_GB))
    diag_values(1, p1)
    p3 = diag_probs(3)
    pos = lax.broadcasted_iota(jnp.int32, (TQ, LANES), 0) + (i * TQ + 1)
    mixed = []
    for g, win in enumerate(POOL_WINDOWS):
        cols = slice(g * GROUP_DIM, (g + 1) * GROUP_DIM)
        wsum = ext_s[:, cols]
        shift = 1
        while shift < win:
            rows = wsum.shape[0] - SUBLANES
            wsum = wsum[SUBLANES:] + wsum[SUBLANES - shift:SUBLANES - shift + rows]
            shift *= 2
        wsum = wsum[wsum.shape[0] - TQ:]
        u = ext_s[HALO:HALO + TQ, cols]
        count = jnp.minimum(pos, win).astype(F32)
        pooled = (wsum / count - u).astype(BF16)
        mixed.append(_dot(pooled, pool_w_ref[g]))
    y_b = (jnp.concatenate(mixed, axis=1) + pool_b_ref[...]) * pool_scale_ref[...]
    ext_s[0:HALO, :] = ext_s[TQ:TQ + HALO, :]
    merge_b = merge_gate(1)
    merged_s[...] = merge_b * _dot((y_b * gate_b).astype(BF16), w_branch_ref[1])

    gate_c = _silu_of_half(proj(OFF_GC))
    diag_values(2, p2)
    pr = lax.shift_right_logical(lax.broadcasted_iota(jnp.int32, (SGU_BLOCK, SGU_BLOCK), 0), 6)
    pc = lax.shift_right_logical(lax.broadcasted_iota(jnp.int32, (SGU_BLOCK, SGU_BLOCK), 1), 6)
    sgu_mask = pc <= pr
    blocks = []
    for nb in range(TQ // SGU_BLOCK):
        rws = slice(nb * SGU_BLOCK, (nb + 1) * SGU_BLOCK)
        groups = []
        for g in range(GROUPS):
            cols = slice(g * GROUP_DIM, (g + 1) * GROUP_DIM)
            wg = jnp.where(sgu_mask, sgu_w_ref[g], 0.0).astype(BF16)
            bias = jnp.broadcast_to(sgu_bt_ref[:, g:g + 1], (SGU_BLOCK, GROUP_DIM))
            groups.append(_dot(wg, svn[rws, cols]) + bias)
        blocks.append(jnp.concatenate(groups, axis=1))
    y_c = su * jnp.concatenate(blocks, axis=0)
    merge_c = merge_gate(2)
    diag_values(3, p3)
    merged_s[...] += merge_c * _dot((y_c * gate_c).astype(BF16), w_branch_ref[2])

    strips = [(hd, cs) for hd in range(HEADS) for cs in col_strips]

    def strip_scores(hd, cs, j):
        return _dot(k_s[hd, j * TQ:(j + 1) * TQ, :], qt_s[hd, :, cs])

    def strip_update(hd, cs, j, s):
        m_prev = m_s[hd, :, cs]
        m_next = jnp.maximum(m_prev, jnp.max(s, axis=0, keepdims=True))
        alpha = jnp.exp2(m_prev - m_next)
        p = jnp.exp2(s - m_next).astype(BF16)
        acc_s[hd, :, cs] = alpha * acc_s[hd, :, cs] + _dot(values_t(hd, j), p)
        m_s[hd, :, cs] = m_next

    pending = []
    for j in range(i):
        for hd, cs in strips:
            pending.append((hd, cs, j, strip_scores(hd, cs, j)))
            if len(pending) > AHEAD:
                strip_update(*pending.pop(0))
    for item in pending:
        strip_update(*item)

    gate_a = _silu_of_half(proj(OFF_GA))
    merge_a = merge_gate(0)
    heads_t = []
    for hd in range(HEADS):
        o = acc_s[hd, 0:V_DIM, :] / acc_s[hd, V_DIM:V_DIM + 1, :]
        o = o[:, :TQ] - lam * o[:, TQ:]
        o = o * lax.rsqrt(jnp.mean(o * o, axis=0, keepdims=True) + EPS) * subln_ref[...]
        heads_t.append(o * (1.0 - lam_init))
    y_a = jnp.concatenate(heads_t, axis=0).T
    merged = merge_a * _dot((y_a * gate_a).astype(BF16), w_branch_ref[0]) + merged_s[...]

    out = _dot(merged.astype(BF16), w_out_ref[...])
    out = out * lax.rsqrt(jnp.mean(out * out, axis=-1, keepdims=True) + EPS) * post_g_ref[...]
    out_ref[0] = x + out


def _full(shape):
    return pl.BlockSpec(shape, lambda b, i: (0,) * len(shape))


def _layer(x, layer_idx, pre_g, post_g, w_in, lq1, lk1, lq2, lk2, subln_g, pool_w, pool_b,
           pool_scale, ln_g, ln_b, sgu_w, sgu_b, w_branch, w_merge, b_merge, w_out):
    batch, seq, d = x.shape
    assert d == D_MODEL and seq == SEQ and seq % TQ == 0 and TQ % SGU_BLOCK == 0
    lam_init = 0.8 - 0.6 * math.exp(-0.3 * layer_idx)
    row = lambda a: a.reshape(1, -1)
    col = jnp.arange(w_in.shape[1])
    is_gate = ((col >= OFF_GA) & (col < OFF_P)) | ((col >= OFF_GB) & (col < OFF_SU)) | (col >= OFF_GC)
    w_in = w_in * jnp.where(is_gate, 0.5, 1.0).astype(w_in.dtype)
    w_merge, b_merge, w_out = 0.5 * w_merge, 0.5 * b_merge, 0.5 * w_out
    operands = (
        x, row(pre_g), row(post_g), w_in.astype(BF16), row(lq1), row(lk1), row(lq2), row(lk2),
        subln_g.reshape(-1, 1), pool_w.astype(BF16), row(pool_b), row(pool_scale), row(ln_g),
        row(ln_b), sgu_w, sgu_b.T, w_branch.astype(BF16), w_merge.astype(BF16), row(b_merge),
        w_out.astype(BF16))
    x_spec = pl.BlockSpec((1, TQ, D_MODEL), lambda b, i: (b, i, 0))
    in_specs = [x_spec] + [_full(a.shape) for a in operands[1:]]
    scratch = [
        pltpu.VMEM((HEADS, LANES, 2 * TQ), BF16),
        pltpu.VMEM((HEADS, SEQ, LANES), BF16),
        pltpu.VMEM((SEQ // TQ, HEADS, V_EXT, TQ), BF16),
        pltpu.VMEM((HEADS, 1, 2 * TQ), F32),
        pltpu.VMEM((HEADS, V_EXT, 2 * TQ), F32),
        pltpu.VMEM((TQ + HALO, WIDTH), F32),
        pltpu.VMEM((TQ, D_MODEL), F32),
    ]
    return pl.pallas_call(
        functools.partial(_layer_kernel, lam_init=lam_init),
        grid=(batch, seq // TQ),
        in_specs=in_specs,
        out_specs=x_spec,
        out_shape=jax.ShapeDtypeStruct(x.shape, x.dtype),
        scratch_shapes=scratch,
        compiler_params=pltpu.CompilerParams(
            dimension_semantics=("arbitrary", "arbitrary"),
            vmem_limit_bytes=VMEM_LIMIT_BYTES),
        name=f"hybrid_layer_{layer_idx}",
    )(*operands)


def kernel(x, pre_norm_g, post_norm_g, w_in, lambda_q1, lambda_k1, lambda_q2, lambda_k2, attn_subln_g, pool_w, pool_b, pool_scale, sgu_ln_g, sgu_ln_b, sgu_w, sgu_b, w_branch, w_merge, b_merge, w_out):
    for l in range(DEPTH):
        x = _layer(x, l, pre_norm_g[l], post_norm_g[l], w_in[l], lambda_q1[l], lambda_k1[l],
                   lambda_q2[l], lambda_k2[l], attn_subln_g[l], pool_w[l], pool_b[l],
                   pool_scale[l], sgu_ln_g[l], sgu_ln_b[l], sgu_w[l], sgu_b[l], w_branch[l],
                   w_merge[l], b_merge[l], w_out[l])
    return x
```

```python
import functools
import math

import jax
import jax.numpy as jnp
from jax import lax
from jax.experimental import pallas as pl
from jax.experimental.pallas import tpu as pltpu

D_MODEL = 1024
SEQ = 2048
DEPTH = 2
CHUNK = 64
WIDTH = D_MODEL // 2
HEADS = 4
QK_DIM = 64
V_DIM = 128
V_EXT = V_DIM + 16
POOL_WINDOWS = (2, 4, 8, 16)
GROUPS = 4
GROUP_DIM = WIDTH // GROUPS
SGU_BLOCK = 128
N_BRANCH = 3
EPS = 1e-6
LANES = 128
SUBLANES = 8
HALO = SUBLANES * (max(POOL_WINDOWS).bit_length() - 1)
TQ = 512
COL_STRIP = 256
AHEAD = 8
VMEM_LIMIT_BYTES = 60 * 1024 * 1024

OFF_Q, OFF_K, OFF_V, OFF_GA, OFF_P, OFF_GB, OFF_SU, OFF_SV, OFF_GC = (
    0, 512, 1024, 1536, 2048, 2560, 3072, 3584, 4096)

F32 = jnp.float32
BF16 = jnp.bfloat16


def _dot(a, b):
    return jnp.dot(a, b, preferred_element_type=F32)


def _silu_of_half(xh):
    return xh * (1.0 + jnp.tanh(xh))


def _twice_sigmoid_of_half(xh):
    return 1.0 + jnp.tanh(xh)


def _gelu(x):
    return 0.5 * x * (1.0 + lax.erf(x * math.sqrt(0.5)))


N_LAYER_INPUTS = 20
N_NEXT_WEIGHTS = 4


def _layer_kernel(*refs, lam_init, convert_next):
    (x_ref, pre_g_ref, post_g_ref, w_in_ref, lq1_ref, lk1_ref, lq2_ref, lk2_ref, subln_ref,
     pool_w_ref, pool_b_ref, pool_scale_ref, ln_g_ref, ln_b_ref, sgu_w_ref, sgu_bt_ref,
     w_branch_ref, w_merge_ref, b_merge_ref, w_out_ref) = refs[:N_LAYER_INPUTS]
    refs = refs[N_LAYER_INPUTS:]
    if convert_next:
        scales, raw, refs = refs[:N_NEXT_WEIGHTS], refs[N_NEXT_WEIGHTS:2 * N_NEXT_WEIGHTS], \
            refs[2 * N_NEXT_WEIGHTS:]
        out_ref, cast, refs = refs[0], refs[1:1 + N_NEXT_WEIGHTS], refs[1 + N_NEXT_WEIGHTS:]
        for scale_ref, raw_ref, cast_ref in zip(scales, raw, cast):
            cast_ref[...] = (raw_ref[...] * scale_ref[...]).astype(BF16)
    else:
        out_ref, refs = refs[0], refs[1:]
    qt_s, k_s, vt_s, m_s, acc_s, ext_s, merged_s = refs

    i = pl.program_id(1)

    @pl.when(i == 0)
    def _():
        ext_s[0:HALO, :] = jnp.zeros((HALO, WIDTH), F32)

    x = x_ref[0]

    h = x * lax.rsqrt(jnp.mean(x * x, axis=-1, keepdims=True) + EPS) * pre_g_ref[...]
    hb = h.astype(BF16)

    def proj(off, width=WIDTH):
        return _dot(hb, w_in_ref[:, off:off + width])

    def merge_gate(n):
        cols = slice(n * D_MODEL, (n + 1) * D_MODEL)
        return _twice_sigmoid_of_half(_dot(hb, w_merge_ref[:, cols]) + b_merge_ref[:, cols])

    lam = (jnp.exp(jnp.sum(lq1_ref[...] * lk1_ref[...], axis=-1, keepdims=True))
           - jnp.exp(jnp.sum(lq2_ref[...] * lk2_ref[...], axis=-1, keepdims=True))
           + lam_init)

    rows_i = pl.ds(pl.multiple_of(i * TQ, TQ), TQ)
    q_t = (proj(OFF_Q) * (QK_DIM ** -0.5 * math.log2(math.e))).T
    k = proj(OFF_K).astype(BF16)
    v_t = proj(OFF_V).T
    ones = jnp.ones((V_EXT - V_DIM, TQ), BF16)
    for hd in range(HEADS):
        vt_s[i, hd, 0:V_DIM, :] = v_t[hd * V_DIM:(hd + 1) * V_DIM].astype(BF16)
        vt_s[i, hd, V_DIM:V_EXT, :] = ones
    first_map = lax.broadcasted_iota(jnp.int32, (LANES, TQ), 0) < QK_DIM
    zero = jnp.zeros((LANES, TQ), BF16)
    for hd in range(HEADS):
        cols = slice(hd * LANES, (hd + 1) * LANES)
        q_hd = q_t[cols].astype(BF16)
        qt_s[hd, :, 0:TQ] = jnp.where(first_map, q_hd, zero)
        qt_s[hd, :, TQ:2 * TQ] = jnp.where(first_map, zero, q_hd)
        k_s[hd, rows_i, :] = k[:, cols]

    def values_t(hd, j):
        return vt_s[j, hd]

    col_strips = [slice(c, c + COL_STRIP) for c in range(0, 2 * TQ, COL_STRIP)]

    def visible_keys(cs):
        return cs.start % TQ + COL_STRIP

    def diag_probs(hd):
        probs = []
        for cs in col_strips:
            nk = visible_keys(cs)
            keys = k_s[hd, pl.ds(pl.multiple_of(i * TQ, TQ), nk), :]
            s = _dot(keys, qt_s[hd, :, cs])
            key_chunk = lax.shift_right_logical(
                lax.broadcasted_iota(jnp.int32, (nk, COL_STRIP), 0), 6)
            qry_chunk = lax.shift_right_logical(
                lax.broadcasted_iota(jnp.int32, (nk, COL_STRIP), 1) + cs.start % TQ, 6)
            s = jnp.where(key_chunk <= qry_chunk, s, -jnp.inf)
            mx = jnp.max(s, axis=0, keepdims=True)
            m_s[hd, :, cs] = mx
            probs.append(jnp.exp2(s - mx).astype(BF16))
        return probs

    def diag_values(hd, probs):
        for cs, p in zip(col_strips, probs):
            acc_s[hd, :, cs] = _dot(vt_s[i, hd, :, 0:visible_keys(cs)], p)

    p0 = diag_probs(0)

    su = _gelu(proj(OFF_SU))
    p1 = diag_probs(1)
    sv = _gelu(proj(OFF_SV))
    diag_values(0, p0)
    mu = jnp.mean(sv, axis=-1, keepdims=True)
    cen = sv - mu
    var = jnp.mean(cen * cen, axis=-1, keepdims=True)
    svn = (cen * lax.rsqrt(var + EPS) * ln_g_ref[...] + ln_b_ref[...]).astype(BF16)
    p2 = diag_probs(2)

    ext_s[HALO:HALO + TQ, :] = proj(OFF_P)
    gate_b = _silu_of_half(proj(OFF_GB))
    diag_values(1, p1)
    p3 = diag_probs(3)
    pos = lax.broadcasted_iota(jnp.int32, (TQ, LANES), 0) + (i * TQ + 1)
    mixed = []
    for g, win in enumerate(POOL_WINDOWS):
        cols = slice(g * GROUP_DIM, (g + 1) * GROUP_DIM)
        wsum = ext_s[:, cols]
        shift = 1
        while shift < win:
            rows = wsum.shape[0] - SUBLANES
            wsum = wsum[SUBLANES:] + wsum[SUBLANES - shift:SUBLANES - shift + rows]
            shift *= 2
        wsum = wsum[wsum.shape[0] - TQ:]
        u = ext_s[HALO:HALO + TQ, cols]
        count = jnp.minimum(pos, win).astype(F32)
        pooled = (wsum / count - u).astype(BF16)
        mixed.append(_dot(pooled, pool_w_ref[g]))
    y_b = (jnp.concatenate(mixed, axis=1) + pool_b_ref[...]) * pool_scale_ref[...]
    ext_s[0:HALO, :] = ext_s[TQ:TQ + HALO, :]
    merge_b = merge_gate(1)
    merged_s[...] = merge_b * _dot((y_b * gate_b).astype(BF16), w_branch_ref[1])

    gate_c = _silu_of_half(proj(OFF_GC))
    diag_values(2, p2)
    pr = lax.shift_right_logical(lax.broadcasted_iota(jnp.int32, (SGU_BLOCK, SGU_BLOCK), 0), 6)
    pc = lax.shift_right_logical(lax.broadcasted_iota(jnp.int32, (SGU_BLOCK, SGU_BLOCK), 1), 6)
    sgu_mask = pc <= pr
    blocks = []
    for nb in range(TQ // SGU_BLOCK):
        rws = slice(nb * SGU_BLOCK, (nb + 1) * SGU_BLOCK)
        groups = []
        for g in range(GROUPS):
            cols = slice(g * GROUP_DIM, (g + 1) * GROUP_DIM)
            wg = jnp.where(sgu_mask, sgu_w_ref[g], 0.0).astype(BF16)
            bias = jnp.broadcast_to(sgu_bt_ref[:, g:g + 1], (SGU_BLOCK, GROUP_DIM))
            groups.append(_dot(wg, svn[rws, cols]) + bias)
        blocks.append(jnp.concatenate(groups, axis=1))
    y_c = su * jnp.concatenate(blocks, axis=0)
    merge_c = merge_gate(2)
    diag_values(3, p3)
    merged_s[...] += merge_c * _dot((y_c * gate_c).astype(BF16), w_branch_ref[2])

    strips = [(hd, cs) for hd in range(HEADS) for cs in col_strips]

    def strip_scores(hd, cs, j):
        rows = pl.ds(pl.multiple_of(j * TQ, TQ), TQ)
        return _dot(k_s[hd, rows, :], qt_s[hd, :, cs])

    def strip_update(hd, cs, j, s):
        m_prev = m_s[hd, :, cs]
        m_next = jnp.maximum(m_prev, jnp.max(s, axis=0, keepdims=True))
        alpha = jnp.exp2(m_prev - m_next)
        p = jnp.exp2(s - m_next).astype(BF16)
        acc_s[hd, :, cs] = alpha * acc_s[hd, :, cs] + _dot(values_t(hd, j), p)
        m_s[hd, :, cs] = m_next

    def kv_body(j, c):
        pending = []
        for hd, cs in strips:
            pending.append((hd, cs, strip_scores(hd, cs, j)))
            if len(pending) > AHEAD:
                strip_update(*pending[0][:2], j, pending[0][2])
                pending.pop(0)
        for hd, cs, s in pending:
            strip_update(hd, cs, j, s)
        return c

    lax.fori_loop(0, i, kv_body, 0)

    gate_a = _silu_of_half(proj(OFF_GA))
    merge_a = merge_gate(0)
    heads_t = []
    for hd in range(HEADS):
        o = acc_s[hd, 0:V_DIM, :] / acc_s[hd, V_DIM:V_DIM + 1, :]
        o = o[:, :TQ] - lam * o[:, TQ:]
        o = o * lax.rsqrt(jnp.mean(o * o, axis=0, keepdims=True) + EPS) * subln_ref[...]
        heads_t.append(o * (1.0 - lam_init))
    y_a = jnp.concatenate(heads_t, axis=0).T
    merged = merge_a * _dot((y_a * gate_a).astype(BF16), w_branch_ref[0]) + merged_s[...]

    out = _dot(merged.astype(BF16), w_out_ref[...])
    out = out * lax.rsqrt(jnp.mean(out * out, axis=-1, keepdims=True) + EPS) * post_g_ref[...]
    out_ref[0] = x + out


def _full(shape):
    return pl.BlockSpec(shape, lambda b, i: (0,) * len(shape))


def _weight_scales():
    col = jnp.arange(OFF_GC + WIDTH)
    is_gate = ((col >= OFF_GA) & (col < OFF_P)) | ((col >= OFF_GB) & (col < OFF_SU)) | (col >= OFF_GC)
    return (jnp.where(is_gate, 0.5, 1.0).astype(F32).reshape(1, -1),
            jnp.full((1, N_BRANCH * D_MODEL), 0.5, F32),
            jnp.ones((1, D_MODEL), F32),
            jnp.full((1, D_MODEL), 0.5, F32))


def _layer(x, layer_idx, pre_g, post_g, lq1, lk1, lq2, lk2, subln_g, pool_w, pool_b, pool_scale,
           ln_g, ln_b, sgu_w, sgu_b, b_merge, weights, next_raw_weights):
    batch, seq, d = x.shape
    assert d == D_MODEL and seq == SEQ and seq % TQ == 0 and TQ % SGU_BLOCK == 0
    lam_init = 0.8 - 0.6 * math.exp(-0.3 * layer_idx)
    row = lambda a: a.reshape(1, -1)
    w_in, w_merge, w_branch, w_out = weights
    operands = [
        x, row(pre_g), row(post_g), w_in, row(lq1), row(lk1), row(lq2), row(lk2),
        subln_g.reshape(-1, 1), pool_w.astype(BF16), row(pool_b), row(pool_scale), row(ln_g),
        row(ln_b), sgu_w, sgu_b.T, w_branch.reshape(N_BRANCH, WIDTH, D_MODEL), w_merge,
        row(0.5 * b_merge), w_out]
    assert len(operands) == N_LAYER_INPUTS
    x_spec = pl.BlockSpec((1, TQ, D_MODEL), lambda b, i: (b, i, 0))
    in_specs = [x_spec] + [_full(a.shape) for a in operands[1:]]
    out_specs = [x_spec]
    out_shape = [jax.ShapeDtypeStruct(x.shape, x.dtype)]
    convert_next = next_raw_weights is not None
    if convert_next:
        n_blk = seq // TQ
        n_steps = batch * n_blk
        scales = _weight_scales()
        operands += list(scales) + list(next_raw_weights)
        in_specs += [_full(a.shape) for a in scales]
        for w in next_raw_weights:
            rows, cols = w.shape
            assert rows % (n_steps * 2 * SUBLANES) == 0
            slab = pl.BlockSpec((rows // n_steps, cols), lambda b, i: (b * n_blk + i, 0))
            in_specs.append(slab)
            out_specs.append(slab)
            out_shape.append(jax.ShapeDtypeStruct(w.shape, BF16))
    scratch = [
        pltpu.VMEM((HEADS, LANES, 2 * TQ), BF16),
        pltpu.VMEM((HEADS, SEQ, LANES), BF16),
        pltpu.VMEM((SEQ // TQ, HEADS, V_EXT, TQ), BF16),
        pltpu.VMEM((HEADS, 1, 2 * TQ), F32),
        pltpu.VMEM((HEADS, V_EXT, 2 * TQ), F32),
        pltpu.VMEM((TQ + HALO, WIDTH), F32),
        pltpu.VMEM((TQ, D_MODEL), F32),
    ]
    outs = pl.pallas_call(
        functools.partial(_layer_kernel, lam_init=lam_init, convert_next=convert_next),
        grid=(batch, seq // TQ),
        in_specs=in_specs,
        out_specs=out_specs,
        out_shape=out_shape,
        scratch_shapes=scratch,
        compiler_params=pltpu.CompilerParams(
            dimension_semantics=("arbitrary", "arbitrary"),
            vmem_limit_bytes=VMEM_LIMIT_BYTES),
        name=f"hybrid_layer_{layer_idx}",
    )(*operands)
    return outs[0], (tuple(outs[1:]) if convert_next else None)


def kernel(x, pre_norm_g, post_norm_g, w_in, lambda_q1, lambda_k1, lambda_q2, lambda_k2, attn_subln_g, pool_w, pool_b, pool_scale, sgu_ln_g, sgu_ln_b, sgu_w, sgu_b, w_branch, w_merge, b_merge, w_out):
    def raw_weights(l):
        return (w_in[l], w_merge[l], w_branch[l].reshape(N_BRANCH * WIDTH, D_MODEL), w_out[l])

    weights = tuple((w * s).astype(BF16) for w, s in zip(raw_weights(0), _weight_scales()))
    for l in range(DEPTH):
        x, weights = _layer(
            x, l, pre_norm_g[l], post_norm_g[l], lambda_q1[l], lambda_k1[l], lambda_q2[l],
            lambda_k2[l], attn_subln_g[l], pool_w[l], pool_b[l], pool_scale[l], sgu_ln_g[l],
            sgu_ln_b[l], sgu_w[l], sgu_b[l], b_merge[l], weights,
            raw_weights(l + 1) if l + 1 < DEPTH else None)
    return x
```

```python
import functools
import math

import jax
import jax.numpy as jnp
from jax import lax
from jax.experimental import pallas as pl
from jax.experimental.pallas import tpu as pltpu

D_MODEL = 1024
SEQ = 2048
DEPTH = 2
CHUNK = 64
WIDTH = D_MODEL // 2
HEADS = 4
QK_DIM = 64
V_DIM = 128
V_EXT = V_DIM + 16
POOL_WINDOWS = (2, 4, 8, 16)
GROUPS = 4
GROUP_DIM = WIDTH // GROUPS
SGU_BLOCK = 128
N_BRANCH = 3
EPS = 1e-6
LANES = 128
SUBLANES = 8
HALO = SUBLANES * (max(POOL_WINDOWS).bit_length() - 1)
TQ = 512
COL_STRIP = 256
AHEAD = 8
VMEM_LIMIT_BYTES = 60 * 1024 * 1024

OFF_Q, OFF_K, OFF_V, OFF_GA, OFF_P, OFF_GB, OFF_SU, OFF_SV, OFF_GC = (
    0, 512, 1024, 1536, 2048, 2560, 3072, 3584, 4096)

F32 = jnp.float32
BF16 = jnp.bfloat16


def _dot(a, b):
    return jnp.dot(a, b, preferred_element_type=F32)


def _silu_of_half(xh):
    return xh * (1.0 + jnp.tanh(xh))


def _twice_sigmoid_of_half(xh):
    return 1.0 + jnp.tanh(xh)


def _gelu(x):
    return 0.5 * x * (1.0 + lax.erf(x * math.sqrt(0.5)))


N_LAYER_INPUTS = 20
N_NEXT_WEIGHTS = 4


def _layer_kernel(*refs, lam_init, convert_next):
    (x_ref, pre_g_ref, post_g_ref, w_in_ref, lq1_ref, lk1_ref, lq2_ref, lk2_ref, subln_ref,
     pool_w_ref, pool_b_ref, pool_scale_ref, ln_g_ref, ln_b_ref, sgu_w_ref, sgu_bt_ref,
     w_branch_ref, w_merge_ref, b_merge_ref, w_out_ref) = refs[:N_LAYER_INPUTS]
    refs = refs[N_LAYER_INPUTS:]
    if convert_next:
        scales, raw, refs = refs[:N_NEXT_WEIGHTS], refs[N_NEXT_WEIGHTS:2 * N_NEXT_WEIGHTS], \
            refs[2 * N_NEXT_WEIGHTS:]
        out_ref, cast, refs = refs[0], refs[1:1 + N_NEXT_WEIGHTS], refs[1 + N_NEXT_WEIGHTS:]
        for scale_ref, raw_ref, cast_ref in zip(scales, raw, cast):
            cast_ref[...] = (raw_ref[0] * scale_ref[...]).astype(BF16)
    else:
        out_ref, refs = refs[0], refs[1:]
    qt_s, k_s, vt_s, m_s, acc_s, ext_s, merged_s = refs

    i = pl.program_id(1)

    @pl.when(i == 0)
    def _():
        ext_s[0:HALO, :] = jnp.zeros((HALO, WIDTH), F32)

    x = x_ref[0]

    h = x * lax.rsqrt(jnp.mean(x * x, axis=-1, keepdims=True) + EPS) * pre_g_ref[...]
    hb = h.astype(BF16)

    def proj(off, width=WIDTH):
        return _dot(hb, w_in_ref[:, off:off + width])

    def merge_gate(n):
        cols = slice(n * D_MODEL, (n + 1) * D_MODEL)
        return _twice_sigmoid_of_half(_dot(hb, w_merge_ref[:, cols]) + b_merge_ref[:, cols])

    lam = (jnp.exp(jnp.sum(lq1_ref[...] * lk1_ref[...], axis=-1, keepdims=True))
           - jnp.exp(jnp.sum(lq2_ref[...] * lk2_ref[...], axis=-1, keepdims=True))
           + lam_init)

    rows_i = pl.ds(pl.multiple_of(i * TQ, TQ), TQ)
    q_t = (proj(OFF_Q) * (QK_DIM ** -0.5 * math.log2(math.e))).T
    k = proj(OFF_K).astype(BF16)
    v_t = proj(OFF_V).T
    ones = jnp.ones((V_EXT - V_DIM, TQ), BF16)
    for hd in range(HEADS):
        vt_s[i, hd, 0:V_DIM, :] = v_t[hd * V_DIM:(hd + 1) * V_DIM].astype(BF16)
        vt_s[i, hd, V_DIM:V_EXT, :] = ones
    first_map = lax.broadcasted_iota(jnp.int32, (LANES, TQ), 0) < QK_DIM
    zero = jnp.zeros((LANES, TQ), BF16)
    for hd in range(HEADS):
        cols = slice(hd * LANES, (hd + 1) * LANES)
        q_hd = q_t[cols].astype(BF16)
        qt_s[hd, :, 0:TQ] = jnp.where(first_map, q_hd, zero)
        qt_s[hd, :, TQ:2 * TQ] = jnp.where(first_map, zero, q_hd)
        k_s[hd, rows_i, :] = k[:, cols]

    def values_t(hd, j):
        return vt_s[j, hd]

    col_strips = [slice(c, c + COL_STRIP) for c in range(0, 2 * TQ, COL_STRIP)]

    def visible_keys(cs):
        return cs.start % TQ + COL_STRIP

    def diag_probs(hd):
        probs = []
        for cs in col_strips:
            nk = visible_keys(cs)
            keys = k_s[hd, pl.ds(pl.multiple_of(i * TQ, TQ), nk), :]
            s = _dot(keys, qt_s[hd, :, cs])
            key_chunk = lax.shift_right_logical(
                lax.broadcasted_iota(jnp.int32, (nk, COL_STRIP), 0), 6)
            qry_chunk = lax.shift_right_logical(
                lax.broadcasted_iota(jnp.int32, (nk, COL_STRIP), 1) + cs.start % TQ, 6)
            s = jnp.where(key_chunk <= qry_chunk, s, -jnp.inf)
            mx = jnp.max(s, axis=0, keepdims=True)
            m_s[hd, :, cs] = mx
            probs.append(jnp.exp2(s - mx).astype(BF16))
        return probs

    def diag_values(hd, probs):
        for cs, p in zip(col_strips, probs):
            acc_s[hd, :, cs] = _dot(vt_s[i, hd, :, 0:visible_keys(cs)], p)

    p0 = diag_probs(0)

    su = _gelu(proj(OFF_SU))
    p1 = diag_probs(1)
    sv = _gelu(proj(OFF_SV))
    diag_values(0, p0)
    mu = jnp.mean(sv, axis=-1, keepdims=True)
    cen = sv - mu
    var = jnp.mean(cen * cen, axis=-1, keepdims=True)
    svn = (cen * lax.rsqrt(var + EPS) * ln_g_ref[...] + ln_b_ref[...]).astype(BF16)
    p2 = diag_probs(2)

    ext_s[HALO:HALO + TQ, :] = proj(OFF_P)
    gate_b = _silu_of_half(proj(OFF_GB))
    diag_values(1, p1)
    p3 = diag_probs(3)
    pos = lax.broadcasted_iota(jnp.int32, (TQ, LANES), 0) + (i * TQ + 1)
    mixed = []
    for g, win in enumerate(POOL_WINDOWS):
        cols = slice(g * GROUP_DIM, (g + 1) * GROUP_DIM)
        wsum = ext_s[:, cols]
        shift = 1
        while shift < win:
            rows = wsum.shape[0] - SUBLANES
            wsum = wsum[SUBLANES:] + wsum[SUBLANES - shift:SUBLANES - shift + rows]
            shift *= 2
        wsum = wsum[wsum.shape[0] - TQ:]
        u = ext_s[HALO:HALO + TQ, cols]
        count = jnp.minimum(pos, win).astype(F32)
        pooled = (wsum / count - u).astype(BF16)
        mixed.append(_dot(pooled, pool_w_ref[g]))
    y_b = (jnp.concatenate(mixed, axis=1) + pool_b_ref[...]) * pool_scale_ref[...]
    ext_s[0:HALO, :] = ext_s[TQ:TQ + HALO, :]
    merge_b = merge_gate(1)
    merged_s[...] = merge_b * _dot((y_b * gate_b).astype(BF16), w_branch_ref[1])

    gate_c = _silu_of_half(proj(OFF_GC))
    diag_values(2, p2)
    pr = lax.shift_right_logical(lax.broadcasted_iota(jnp.int32, (SGU_BLOCK, SGU_BLOCK), 0), 6)
    pc = lax.shift_right_logical(lax.broadcasted_iota(jnp.int32, (SGU_BLOCK, SGU_BLOCK), 1), 6)
    sgu_mask = pc <= pr
    blocks = []
    for nb in range(TQ // SGU_BLOCK):
        rws = slice(nb * SGU_BLOCK, (nb + 1) * SGU_BLOCK)
        groups = []
        for g in range(GROUPS):
            cols = slice(g * GROUP_DIM, (g + 1) * GROUP_DIM)
            wg = jnp.where(sgu_mask, sgu_w_ref[g], 0.0).astype(BF16)
            bias = jnp.broadcast_to(sgu_bt_ref[:, g:g + 1], (SGU_BLOCK, GROUP_DIM))
            groups.append(_dot(wg, svn[rws, cols]) + bias)
        blocks.append(jnp.concatenate(groups, axis=1))
    y_c = su * jnp.concatenate(blocks, axis=0)
    merge_c = merge_gate(2)
    diag_values(3, p3)
    merged_s[...] += merge_c * _dot((y_c * gate_c).astype(BF16), w_branch_ref[2])

    strips = [(hd, cs) for hd in range(HEADS) for cs in col_strips]

    def strip_scores(hd, cs, j):
        rows = pl.ds(pl.multiple_of(j * TQ, TQ), TQ)
        return _dot(k_s[hd, rows, :], qt_s[hd, :, cs])

    def strip_update(hd, cs, j, s):
        m_prev = m_s[hd, :, cs]
        m_next = jnp.maximum(m_prev, jnp.max(s, axis=0, keepdims=True))
        alpha = jnp.exp2(m_prev - m_next)
        p = jnp.exp2(s - m_next).astype(BF16)
        acc_s[hd, :, cs] = alpha * acc_s[hd, :, cs] + _dot(values_t(hd, j), p)
        m_s[hd, :, cs] = m_next

    def kv_body(j, c):
        pending = []
        for hd, cs in strips:
            pending.append((hd, cs, strip_scores(hd, cs, j)))
            if len(pending) > AHEAD:
                strip_update(*pending[0][:2], j, pending[0][2])
                pending.pop(0)
        for hd, cs, s in pending:
            strip_update(hd, cs, j, s)
        return c

    lax.fori_loop(0, i, kv_body, 0)

    gate_a = _silu_of_half(proj(OFF_GA))
    merge_a = merge_gate(0)
    heads_t = []
    for hd in range(HEADS):
        o = acc_s[hd, 0:V_DIM, :] / acc_s[hd, V_DIM:V_DIM + 1, :]
        o = o[:, :TQ] - lam * o[:, TQ:]
        o = o * lax.rsqrt(jnp.mean(o * o, axis=0, keepdims=True) + EPS) * subln_ref[...]
        heads_t.append(o * (1.0 - lam_init))
    y_a = jnp.concatenate(heads_t, axis=0).T
    merged = merge_a * _dot((y_a * gate_a).astype(BF16), w_branch_ref[0]) + merged_s[...]

    out = _dot(merged.astype(BF16), w_out_ref[...])
    out = out * lax.rsqrt(jnp.mean(out * out, axis=-1, keepdims=True) + EPS) * post_g_ref[...]
    out_ref[0] = x + out


def _full(shape):
    return pl.BlockSpec(shape, lambda b, i: (0,) * len(shape))


def _weight_scales():
    col = jnp.arange(OFF_GC + WIDTH)
    is_gate = ((col >= OFF_GA) & (col < OFF_P)) | ((col >= OFF_GB) & (col < OFF_SU)) | (col >= OFF_GC)
    return (jnp.where(is_gate, 0.5, 1.0).astype(F32).reshape(1, -1),
            jnp.full((1, N_BRANCH * D_MODEL), 0.5, F32),
            jnp.ones((1, D_MODEL), F32),
            jnp.full((1, D_MODEL), 0.5, F32))


def _layer(x, layer_idx, pre_g, post_g, lq1, lk1, lq2, lk2, subln_g, pool_w, pool_b, pool_scale,
           ln_g, ln_b, sgu_w, sgu_b, b_merge, weights, stacked_raw_weights):
    batch, seq, d = x.shape
    assert d == D_MODEL and seq == SEQ and seq % TQ == 0 and TQ % SGU_BLOCK == 0
    lam_init = 0.8 - 0.6 * math.exp(-0.3 * layer_idx)
    row = lambda a: a.reshape(1, -1)
    w_in, w_merge, w_branch, w_out = weights
    operands = [
        x, row(pre_g), row(post_g), w_in, row(lq1), row(lk1), row(lq2), row(lk2),
        subln_g.reshape(-1, 1), pool_w.astype(BF16), row(pool_b), row(pool_scale), row(ln_g),
        row(ln_b), sgu_w, sgu_b.T, w_branch.reshape(N_BRANCH, WIDTH, D_MODEL), w_merge,
        row(0.5 * b_merge), w_out]
    assert len(operands) == N_LAYER_INPUTS
    x_spec = pl.BlockSpec((1, TQ, D_MODEL), lambda b, i: (b, i, 0))
    in_specs = [x_spec] + [_full(a.shape) for a in operands[1:]]
    out_specs = [x_spec]
    out_shape = [jax.ShapeDtypeStruct(x.shape, x.dtype)]
    convert_next = layer_idx + 1 < DEPTH
    if convert_next:
        n_blk = seq // TQ
        n_steps = batch * n_blk
        scales = _weight_scales()
        operands += list(scales) + list(stacked_raw_weights)
        in_specs += [_full(a.shape) for a in scales]
        for w in stacked_raw_weights:
            _, rows, cols = w.shape
            assert rows % (n_steps * 2 * SUBLANES) == 0
            slab = rows // n_steps
            in_specs.append(pl.BlockSpec((1, slab, cols),
                                         lambda b, i: (layer_idx + 1, b * n_blk + i, 0)))
            out_specs.append(pl.BlockSpec((slab, cols), lambda b, i: (b * n_blk + i, 0)))
            out_shape.append(jax.ShapeDtypeStruct((rows, cols), BF16))
    scratch = [
        pltpu.VMEM((HEADS, LANES, 2 * TQ), BF16),
        pltpu.VMEM((HEADS, SEQ, LANES), BF16),
        pltpu.VMEM((SEQ // TQ, HEADS, V_EXT, TQ), BF16),
        pltpu.VMEM((HEADS, 1, 2 * TQ), F32),
        pltpu.VMEM((HEADS, V_EXT, 2 * TQ), F32),
        pltpu.VMEM((TQ + HALO, WIDTH), F32),
        pltpu.VMEM((TQ, D_MODEL), F32),
    ]
    outs = pl.pallas_call(
        functools.partial(_layer_kernel, lam_init=lam_init, convert_next=convert_next),
        grid=(batch, seq // TQ),
        in_specs=in_specs,
        out_specs=out_specs,
        out_shape=out_shape,
        scratch_shapes=scratch,
        compiler_params=pltpu.CompilerParams(
            dimension_semantics=("arbitrary", "arbitrary"),
            vmem_limit_bytes=VMEM_LIMIT_BYTES),
        name=f"hybrid_layer_{layer_idx}",
    )(*operands)
    return outs[0], (tuple(outs[1:]) if convert_next else None)


def kernel(x, pre_norm_g, post_norm_g, w_in, lambda_q1, lambda_k1, lambda_q2, lambda_k2, attn_subln_g, pool_w, pool_b, pool_scale, sgu_ln_g, sgu_ln_b, sgu_w, sgu_b, w_branch, w_merge, b_merge, w_out):
    stacked = (w_in, w_merge, w_branch.reshape(DEPTH, N_BRANCH * WIDTH, D_MODEL), w_out)
    weights = tuple((w[0] * s).astype(BF16) for w, s in zip(stacked, _weight_scales()))
    for l in range(DEPTH):
        x, weights = _layer(
            x, l, pre_norm_g[l], post_norm_g[l], lambda_q1[l], lambda_k1[l], lambda_q2[l],
            lambda_k2[l], attn_subln_g[l], pool_w[l], pool_b[l], pool_scale[l], sgu_ln_g[l],
            sgu_ln_b[l], sgu_w[l], sgu_b[l], b_merge[l], weights, stacked)
    return x
```

```python
import functools
import math

import jax
import jax.numpy as jnp
from jax import lax
from jax.experimental import pallas as pl
from jax.experimental.pallas import tpu as pltpu

D_MODEL = 1024
SEQ = 2048
DEPTH = 2
CHUNK = 64
CHUNK_SHIFT = CHUNK.bit_length() - 1
WIDTH = D_MODEL // 2
HEADS = 4
QK_DIM = 64
V_DIM = 128
V_EXT = V_DIM + 16
POOL_WINDOWS = (2, 4, 8, 16)
GROUPS = 4
GROUP_DIM = WIDTH // GROUPS
SGU_BLOCK = 128
N_BRANCH = 3
EPS = 1e-6
LANES = 128
SUBLANES = 8
HALO = SUBLANES * (max(POOL_WINDOWS).bit_length() - 1)
TQ = 512
COL_STRIP = 256
AHEAD = 8
V7X_VMEM_BYTES = 64 * 1024 * 1024
VMEM_LIMIT_BYTES = V7X_VMEM_BYTES - V7X_VMEM_BYTES // 16

OFF_Q, OFF_K, OFF_V, OFF_GA, OFF_P, OFF_GB, OFF_SU, OFF_SV, OFF_GC = (
    0, 512, 1024, 1536, 2048, 2560, 3072, 3584, 4096)

F32 = jnp.float32
BF16 = jnp.bfloat16


def _dot(a, b):
    return jnp.dot(a, b, preferred_element_type=F32)


def _silu_of_half(xh):
    return xh * (1.0 + jnp.tanh(xh))


def _twice_sigmoid_of_half(xh):
    return 1.0 + jnp.tanh(xh)


def _gelu(x):
    return 0.5 * x * (1.0 + lax.erf(x * math.sqrt(0.5)))


N_LAYER_INPUTS = 20
N_NEXT_WEIGHTS = 4


def _layer_kernel(*refs, lam_init, convert_next):
    (x_ref, pre_g_ref, post_g_ref, w_in_ref, lq1_ref, lk1_ref, lq2_ref, lk2_ref, subln_ref,
     pool_w_ref, pool_b_ref, pool_scale_ref, ln_g_ref, ln_b_ref, sgu_w_ref, sgu_bt_ref,
     w_branch_ref, w_merge_ref, b_merge_ref, w_out_ref) = refs[:N_LAYER_INPUTS]
    refs = refs[N_LAYER_INPUTS:]
    if convert_next:
        scales, raw, refs = refs[:N_NEXT_WEIGHTS], refs[N_NEXT_WEIGHTS:2 * N_NEXT_WEIGHTS], \
            refs[2 * N_NEXT_WEIGHTS:]
        out_ref, cast, refs = refs[0], refs[1:1 + N_NEXT_WEIGHTS], refs[1 + N_NEXT_WEIGHTS:]
        for scale_ref, raw_ref, cast_ref in zip(scales, raw, cast):
            cast_ref[...] = (raw_ref[0] * scale_ref[...]).astype(BF16)
    else:
        out_ref, refs = refs[0], refs[1:]
    qt_s, k_s, vt_s, m_s, acc_s, ext_s, merged_s = refs

    i = pl.program_id(1)

    @pl.when(i == 0)
    def _():
        ext_s[0:HALO, :] = jnp.zeros((HALO, WIDTH), F32)

    x = x_ref[0]

    h = x * lax.rsqrt(jnp.mean(x * x, axis=-1, keepdims=True) + EPS) * pre_g_ref[0]
    hb = h.astype(BF16)

    def proj(off, width=WIDTH):
        return _dot(hb, w_in_ref[:, off:off + width])

    def merge_gate(n):
        cols = slice(n * D_MODEL, (n + 1) * D_MODEL)
        return _twice_sigmoid_of_half(_dot(hb, w_merge_ref[:, cols]) + b_merge_ref[0, :, cols])

    lam = (jnp.exp(jnp.sum(lq1_ref[0] * lk1_ref[0], axis=-1, keepdims=True))
           - jnp.exp(jnp.sum(lq2_ref[0] * lk2_ref[0], axis=-1, keepdims=True))
           + lam_init)

    rows_i = pl.ds(pl.multiple_of(i * TQ, TQ), TQ)
    q_t = (proj(OFF_Q) * (QK_DIM ** -0.5 * math.log2(math.e))).T
    k = proj(OFF_K).astype(BF16)
    v_t = proj(OFF_V).T
    ones = jnp.ones((V_EXT - V_DIM, TQ), BF16)
    for hd in range(HEADS):
        vt_s[i, hd, 0:V_DIM, :] = v_t[hd * V_DIM:(hd + 1) * V_DIM].astype(BF16)
        vt_s[i, hd, V_DIM:V_EXT, :] = ones
    first_map = lax.broadcasted_iota(jnp.int32, (LANES, TQ), 0) < QK_DIM
    zero = jnp.zeros((LANES, TQ), BF16)
    for hd in range(HEADS):
        cols = slice(hd * LANES, (hd + 1) * LANES)
        q_hd = q_t[cols].astype(BF16)
        qt_s[hd, :, 0:TQ] = jnp.where(first_map, q_hd, zero)
        qt_s[hd, :, TQ:2 * TQ] = jnp.where(first_map, zero, q_hd)
        k_s[hd, rows_i, :] = k[:, cols]

    def values_t(hd, j):
        return vt_s[j, hd]

    col_strips = [slice(c, c + COL_STRIP) for c in range(0, 2 * TQ, COL_STRIP)]

    def visible_keys(cs):
        return cs.start % TQ + COL_STRIP

    def diag_probs(hd):
        probs = []
        for cs in col_strips:
            nk = visible_keys(cs)
            keys = k_s[hd, pl.ds(pl.multiple_of(i * TQ, TQ), nk), :]
            s = _dot(keys, qt_s[hd, :, cs])
            key_chunk = lax.shift_right_logical(
                lax.broadcasted_iota(jnp.int32, (nk, COL_STRIP), 0), CHUNK_SHIFT)
            qry_chunk = lax.shift_right_logical(
                lax.broadcasted_iota(jnp.int32, (nk, COL_STRIP), 1) + cs.start % TQ, CHUNK_SHIFT)
            s = jnp.where(key_chunk <= qry_chunk, s, -jnp.inf)
            mx = jnp.max(s, axis=0, keepdims=True)
            m_s[hd, :, cs] = mx
            probs.append(jnp.exp2(s - mx).astype(BF16))
        return probs

    def diag_values(hd, probs):
        for cs, p in zip(col_strips, probs):
            acc_s[hd, :, cs] = _dot(vt_s[i, hd, :, 0:visible_keys(cs)], p)

    p0 = diag_probs(0)

    su = _gelu(proj(OFF_SU))
    p1 = diag_probs(1)
    sv = _gelu(proj(OFF_SV))
    diag_values(0, p0)
    mu = jnp.mean(sv, axis=-1, keepdims=True)
    cen = sv - mu
    var = jnp.mean(cen * cen, axis=-1, keepdims=True)
    svn = (cen * lax.rsqrt(var + EPS) * ln_g_ref[0] + ln_b_ref[0]).astype(BF16)
    p2 = diag_probs(2)

    ext_s[HALO:HALO + TQ, :] = proj(OFF_P)
    gate_b = _silu_of_half(proj(OFF_GB))
    diag_values(1, p1)
    p3 = diag_probs(3)
    pos = lax.broadcasted_iota(jnp.int32, (TQ, LANES), 0) + (i * TQ + 1)
    mixed = []
    for g, win in enumerate(POOL_WINDOWS):
        cols = slice(g * GROUP_DIM, (g + 1) * GROUP_DIM)
        wsum = ext_s[:, cols]
        shift = 1
        while shift < win:
            rows = wsum.shape[0] - SUBLANES
            wsum = wsum[SUBLANES:] + wsum[SUBLANES - shift:SUBLANES - shift + rows]
            shift *= 2
        wsum = wsum[wsum.shape[0] - TQ:]
        u = ext_s[HALO:HALO + TQ, cols]
        count = jnp.minimum(pos, win).astype(F32)
        pooled = (wsum / count - u).astype(BF16)
        mixed.append(_dot(pooled, pool_w_ref[0, g]))
    y_b = (jnp.concatenate(mixed, axis=1) + pool_b_ref[0]) * pool_scale_ref[0]
    ext_s[0:HALO, :] = ext_s[TQ:TQ + HALO, :]
    merge_b = merge_gate(1)
    merged_s[...] = merge_b * _dot((y_b * gate_b).astype(BF16), w_branch_ref[1])

    gate_c = _silu_of_half(proj(OFF_GC))
    diag_values(2, p2)
    pr = lax.shift_right_logical(lax.broadcasted_iota(jnp.int32, (SGU_BLOCK, SGU_BLOCK), 0), CHUNK_SHIFT)
    pc = lax.shift_right_logical(lax.broadcasted_iota(jnp.int32, (SGU_BLOCK, SGU_BLOCK), 1), CHUNK_SHIFT)
    sgu_mask = pc <= pr
    blocks = []
    for nb in range(TQ // SGU_BLOCK):
        rws = slice(nb * SGU_BLOCK, (nb + 1) * SGU_BLOCK)
        groups = []
        for g in range(GROUPS):
            cols = slice(g * GROUP_DIM, (g + 1) * GROUP_DIM)
            wg = jnp.where(sgu_mask, sgu_w_ref[0, g], 0.0).astype(BF16)
            bias = jnp.broadcast_to(sgu_bt_ref[0, :, g:g + 1], (SGU_BLOCK, GROUP_DIM))
            groups.append(_dot(wg, svn[rws, cols]) + bias)
        blocks.append(jnp.concatenate(groups, axis=1))
    y_c = su * jnp.concatenate(blocks, axis=0)
    merge_c = merge_gate(2)
    diag_values(3, p3)
    merged_s[...] += merge_c * _dot((y_c * gate_c).astype(BF16), w_branch_ref[2])

    strips = [(hd, cs) for hd in range(HEADS) for cs in col_strips]

    def strip_scores(hd, cs, j):
        rows = pl.ds(pl.multiple_of(j * TQ, TQ), TQ)
        return _dot(k_s[hd, rows, :], qt_s[hd, :, cs])

    def strip_update(hd, cs, j, s):
        m_prev = m_s[hd, :, cs]
        m_next = jnp.maximum(m_prev, jnp.max(s, axis=0, keepdims=True))
        alpha = jnp.exp2(m_prev - m_next)
        p = jnp.exp2(s - m_next).astype(BF16)
        acc_s[hd, :, cs] = alpha * acc_s[hd, :, cs] + _dot(values_t(hd, j), p)
        m_s[hd, :, cs] = m_next

    def key_blocks(js):
        pending = []
        for j in js:
            for hd, cs in strips:
                pending.append((hd, cs, j, strip_scores(hd, cs, j)))
                if len(pending) > AHEAD:
                    strip_update(*pending.pop(0))
        for item in pending:
            strip_update(*item)

    def pair_body(jj, c):
        key_blocks((2 * jj, 2 * jj + 1))
        return c

    lax.fori_loop(0, i // 2, pair_body, 0)

    @pl.when(i % 2 == 1)
    def _():
        key_blocks((i - 1,))

    gate_a = _silu_of_half(proj(OFF_GA))
    merge_a = merge_gate(0)
    heads_t = []
    for hd in range(HEADS):
        o = acc_s[hd, 0:V_DIM, :] / acc_s[hd, V_DIM:V_DIM + 1, :]
        o = o[:, :TQ] - lam * o[:, TQ:]
        o = o * lax.rsqrt(jnp.mean(o * o, axis=0, keepdims=True) + EPS) * subln_ref[0]
        heads_t.append(o * (1.0 - lam_init))
    y_a = jnp.concatenate(heads_t, axis=0).T
    merged = merge_a * _dot((y_a * gate_a).astype(BF16), w_branch_ref[0]) + merged_s[...]

    out = _dot(merged.astype(BF16), w_out_ref[...])
    out = out * lax.rsqrt(jnp.mean(out * out, axis=-1, keepdims=True) + EPS) * post_g_ref[0]
    out_ref[0] = x + out


def _full(shape):
    return pl.BlockSpec(shape, lambda b, i: (0,) * len(shape))


def _weight_scales():
    col = jnp.arange(OFF_GC + WIDTH)
    is_gate = ((col >= OFF_GA) & (col < OFF_P)) | ((col >= OFF_GB) & (col < OFF_SU)) | (col >= OFF_GC)
    return (jnp.where(is_gate, 0.5, 1.0).astype(F32).reshape(1, -1),
            jnp.full((1, N_BRANCH * D_MODEL), 0.5, F32),
            jnp.ones((1, D_MODEL), F32),
            jnp.full((1, D_MODEL), 0.5, F32))


def _layer(x, layer_idx, small, weights, stacked_raw_weights):
    batch, seq, d = x.shape
    assert d == D_MODEL and seq == SEQ and seq % TQ == 0 and TQ % SGU_BLOCK == 0
    lam_init = 0.8 - 0.6 * math.exp(-0.3 * layer_idx)
    w_in, w_merge, w_branch, w_out = weights
    (pre_g, post_g, lq1, lk1, lq2, lk2, subln_g, pool_w, pool_b, pool_scale, ln_g, ln_b, sgu_w,
     sgu_bt, b_merge) = small
    operands = [
        x, pre_g, post_g, w_in, lq1, lk1, lq2, lk2, subln_g, pool_w, pool_b, pool_scale, ln_g,
        ln_b, sgu_w, sgu_bt, w_branch.reshape(N_BRANCH, WIDTH, D_MODEL), w_merge, b_merge, w_out]
    assert len(operands) == N_LAYER_INPUTS
    resident = {3, 16, 17, 19}
    x_spec = pl.BlockSpec((1, TQ, D_MODEL), lambda b, i: (b, i, 0))

    def of_this_layer(a):
        block = (1,) + a.shape[1:]
        return pl.BlockSpec(block, lambda b, i: (layer_idx,) + (0,) * (len(block) - 1))

    in_specs = [x_spec] + [_full(a.shape) if n in resident else of_this_layer(a)
                           for n, a in enumerate(operands) if n > 0]
    out_specs = [x_spec]
    out_shape = [jax.ShapeDtypeStruct(x.shape, x.dtype)]
    convert_next = layer_idx + 1 < DEPTH
    if convert_next:
        n_blk = seq // TQ
        n_steps = batch * n_blk
        scales = _weight_scales()
        operands += list(scales) + list(stacked_raw_weights)
        in_specs += [_full(a.shape) for a in scales]
        for w in stacked_raw_weights:
            _, rows, cols = w.shape
            assert rows % (n_steps * 2 * SUBLANES) == 0
            slab = rows // n_steps
            in_specs.append(pl.BlockSpec((1, slab, cols),
                                         lambda b, i: (layer_idx + 1, b * n_blk + i, 0)))
            out_specs.append(pl.BlockSpec((slab, cols), lambda b, i: (b * n_blk + i, 0)))
            out_shape.append(jax.ShapeDtypeStruct((rows, cols), BF16))
    scratch = [
        pltpu.VMEM((HEADS, LANES, 2 * TQ), BF16),
        pltpu.VMEM((HEADS, SEQ, LANES), BF16),
        pltpu.VMEM((SEQ // TQ, HEADS, V_EXT, TQ), BF16),
        pltpu.VMEM((HEADS, 1, 2 * TQ), F32),
        pltpu.VMEM((HEADS, V_EXT, 2 * TQ), F32),
        pltpu.VMEM((TQ + HALO, WIDTH), F32),
        pltpu.VMEM((TQ, D_MODEL), F32),
    ]
    outs = pl.pallas_call(
        functools.partial(_layer_kernel, lam_init=lam_init, convert_next=convert_next),
        grid=(batch, seq // TQ),
        in_specs=in_specs,
        out_specs=out_specs,
        out_shape=out_shape,
        scratch_shapes=scratch,
        compiler_params=pltpu.CompilerParams(
            dimension_semantics=("arbitrary", "arbitrary"),
            vmem_limit_bytes=VMEM_LIMIT_BYTES),
        name=f"hybrid_layer_{layer_idx}",
    )(*operands)
    return outs[0], (tuple(outs[1:]) if convert_next else None)


def kernel(x, pre_norm_g, post_norm_g, w_in, lambda_q1, lambda_k1, lambda_q2, lambda_k2, attn_subln_g, pool_w, pool_b, pool_scale, sgu_ln_g, sgu_ln_b, sgu_w, sgu_b, w_branch, w_merge, b_merge, w_out):
    stacked = (w_in, w_merge, w_branch.reshape(DEPTH, N_BRANCH * WIDTH, D_MODEL), w_out)
    row = lambda a: a.reshape(DEPTH, 1, -1)
    small = (row(pre_norm_g), row(post_norm_g), row(lambda_q1), row(lambda_k1), row(lambda_q2),
             row(lambda_k2), attn_subln_g.reshape(DEPTH, -1, 1), pool_w.astype(BF16), row(pool_b),
             row(pool_scale), row(sgu_ln_g), row(sgu_ln_b), sgu_w, sgu_b.transpose(0, 2, 1),
             row(0.5 * b_merge))
    weights = tuple((w[0] * s).astype(BF16) for w, s in zip(stacked, _weight_scales()))
    for l in range(DEPTH):
        x, weights = _layer(x, l, small, weights, stacked)
    return x
```

```python
import functools
import math

import jax
import jax.numpy as jnp
import numpy as np
from jax import lax
from jax.experimental import pallas as pl
from jax.experimental.pallas import tpu as pltpu

D_MODEL = 1024
SEQ = 2048
DEPTH = 2
CHUNK = 64
CHUNK_SHIFT = CHUNK.bit_length() - 1
WIDTH = D_MODEL // 2
HEADS = 4
QK_DIM = 64
V_DIM = 128
V_EXT = V_DIM + 16
POOL_WINDOWS = (2, 4, 8, 16)
GROUPS = 4
GROUP_DIM = WIDTH // GROUPS
SGU_BLOCK = 128
N_BRANCH = 3
EPS = 1e-6
LANES = 128
SUBLANES = 8
HALO = SUBLANES * (max(POOL_WINDOWS).bit_length() - 1)
TQ = 512
COL_STRIP = 256
AHEAD = 8
V7X_VMEM_BYTES = 64 * 1024 * 1024
VMEM_LIMIT_BYTES = V7X_VMEM_BYTES - V7X_VMEM_BYTES // 16

OFF_Q, OFF_K, OFF_V, OFF_GA, OFF_P, OFF_GB, OFF_SU, OFF_SV, OFF_GC = (
    0, 512, 1024, 1536, 2048, 2560, 3072, 3584, 4096)

F32 = jnp.float32
BF16 = jnp.bfloat16


def _dot(a, b):
    return jnp.dot(a, b, preferred_element_type=F32)


def _silu_of_half(xh):
    return xh * (1.0 + jnp.tanh(xh))


def _twice_sigmoid_of_half(xh):
    return 1.0 + jnp.tanh(xh)


def _gelu(x):
    return 0.5 * x * (1.0 + lax.erf(x * math.sqrt(0.5)))


N_LAYER_INPUTS = 20
N_NEXT_WEIGHTS = 4


def _layer_kernel(*refs, layer, lam_init, convert_next):
    (x_ref, pre_g_ref, post_g_ref, w_in_ref, lq1_ref, lk1_ref, lq2_ref, lk2_ref, subln_ref,
     pool_w_ref, pool_b_ref, pool_scale_ref, ln_g_ref, ln_b_ref, sgu_w_ref, sgu_bt_ref,
     w_branch_ref, w_merge_ref, b_merge_ref, w_out_ref) = refs[:N_LAYER_INPUTS]
    refs = refs[N_LAYER_INPUTS:]
    if convert_next:
        scales, raw, refs = refs[:N_NEXT_WEIGHTS], refs[N_NEXT_WEIGHTS:2 * N_NEXT_WEIGHTS], \
            refs[2 * N_NEXT_WEIGHTS:]
        out_ref, cast, refs = refs[0], refs[1:1 + N_NEXT_WEIGHTS], refs[1 + N_NEXT_WEIGHTS:]
        for scale_ref, raw_ref, cast_ref in zip(scales, raw, cast):
            cast_ref[...] = (raw_ref[0] * scale_ref[...]).astype(BF16)
    else:
        out_ref, refs = refs[0], refs[1:]
    qt_s, k_s, vt_s, m_s, acc_s, ext_s, merged_s = refs

    def row(ref):
        return ref[layer:layer + 1, :]

    i = pl.program_id(1)

    @pl.when(i == 0)
    def _():
        ext_s[0:HALO, :] = jnp.zeros((HALO, WIDTH), F32)

    x = x_ref[0]

    h = x * lax.rsqrt(jnp.mean(x * x, axis=-1, keepdims=True) + EPS) * row(pre_g_ref)
    hb = h.astype(BF16)

    def proj(off, width=WIDTH):
        return _dot(hb, w_in_ref[:, off:off + width])

    def merge_gate(n):
        cols = slice(n * D_MODEL, (n + 1) * D_MODEL)
        logits = _dot(hb, w_merge_ref[:, cols]) + 0.5 * b_merge_ref[layer:layer + 1, cols]
        return _twice_sigmoid_of_half(logits)

    lam = (jnp.exp(jnp.sum(row(lq1_ref) * row(lk1_ref), axis=-1, keepdims=True))
           - jnp.exp(jnp.sum(row(lq2_ref) * row(lk2_ref), axis=-1, keepdims=True))
           + lam_init)

    rows_i = pl.ds(pl.multiple_of(i * TQ, TQ), TQ)
    q_t = (proj(OFF_Q) * (QK_DIM ** -0.5 * math.log2(math.e))).T
    k = proj(OFF_K).astype(BF16)
    v_t = proj(OFF_V).T
    ones = jnp.ones((V_EXT - V_DIM, TQ), BF16)
    for hd in range(HEADS):
        vt_s[i, hd, 0:V_DIM, :] = v_t[hd * V_DIM:(hd + 1) * V_DIM].astype(BF16)
        vt_s[i, hd, V_DIM:V_EXT, :] = ones
    first_map = lax.broadcasted_iota(jnp.int32, (LANES, TQ), 0) < QK_DIM
    zero = jnp.zeros((LANES, TQ), BF16)
    for hd in range(HEADS):
        cols = slice(hd * LANES, (hd + 1) * LANES)
        q_hd = q_t[cols].astype(BF16)
        qt_s[hd, :, 0:TQ] = jnp.where(first_map, q_hd, zero)
        qt_s[hd, :, TQ:2 * TQ] = jnp.where(first_map, zero, q_hd)
        k_s[hd, rows_i, :] = k[:, cols]

    def values_t(hd, j):
        return vt_s[j, hd]

    col_strips = [slice(c, c + COL_STRIP) for c in range(0, 2 * TQ, COL_STRIP)]

    def visible_keys(cs):
        return cs.start % TQ + COL_STRIP

    def diag_probs(hd):
        probs = []
        for cs in col_strips:
            nk = visible_keys(cs)
            keys = k_s[hd, pl.ds(pl.multiple_of(i * TQ, TQ), nk), :]
            s = _dot(keys, qt_s[hd, :, cs])
            key_chunk = lax.shift_right_logical(
                lax.broadcasted_iota(jnp.int32, (nk, COL_STRIP), 0), CHUNK_SHIFT)
            qry_chunk = lax.shift_right_logical(
                lax.broadcasted_iota(jnp.int32, (nk, COL_STRIP), 1) + cs.start % TQ, CHUNK_SHIFT)
            s = jnp.where(key_chunk <= qry_chunk, s, -jnp.inf)
            mx = jnp.max(s, axis=0, keepdims=True)
            m_s[hd, :, cs] = mx
            probs.append(jnp.exp2(s - mx).astype(BF16))
        return probs

    def diag_values(hd, probs):
        for cs, p in zip(col_strips, probs):
            acc_s[hd, :, cs] = _dot(vt_s[i, hd, :, 0:visible_keys(cs)], p)

    p0 = diag_probs(0)

    su = _gelu(proj(OFF_SU))
    p1 = diag_probs(1)
    sv = _gelu(proj(OFF_SV))
    diag_values(0, p0)
    mu = jnp.mean(sv, axis=-1, keepdims=True)
    cen = sv - mu
    var = jnp.mean(cen * cen, axis=-1, keepdims=True)
    svn = (cen * lax.rsqrt(var + EPS) * row(ln_g_ref) + row(ln_b_ref)).astype(BF16)
    p2 = diag_probs(2)

    ext_s[HALO:HALO + TQ, :] = proj(OFF_P)
    gate_b = _silu_of_half(proj(OFF_GB))
    diag_values(1, p1)
    p3 = diag_probs(3)
    pos = lax.broadcasted_iota(jnp.int32, (TQ, LANES), 0) + (i * TQ + 1)
    mixed = []
    for g, win in enumerate(POOL_WINDOWS):
        cols = slice(g * GROUP_DIM, (g + 1) * GROUP_DIM)
        wsum = ext_s[:, cols]
        shift = 1
        while shift < win:
            rows = wsum.shape[0] - SUBLANES
            wsum = wsum[SUBLANES:] + wsum[SUBLANES - shift:SUBLANES - shift + rows]
            shift *= 2
        wsum = wsum[wsum.shape[0] - TQ:]
        u = ext_s[HALO:HALO + TQ, cols]
        count = jnp.minimum(pos, win).astype(F32)
        pooled = (wsum / count - u).astype(BF16)
        mixed.append(_dot(pooled, pool_w_ref[0, g].astype(BF16)) + pool_b_ref[layer, g:g + 1, :])
    y_b = jnp.concatenate(mixed, axis=1) * row(pool_scale_ref)
    ext_s[0:HALO, :] = ext_s[TQ:TQ + HALO, :]
    merge_b = merge_gate(1)
    merged_s[...] = merge_b * _dot((y_b * gate_b).astype(BF16), w_branch_ref[1])

    gate_c = _silu_of_half(proj(OFF_GC))
    diag_values(2, p2)
    pr = lax.shift_right_logical(lax.broadcasted_iota(jnp.int32, (SGU_BLOCK, SGU_BLOCK), 0), CHUNK_SHIFT)
    pc = lax.shift_right_logical(lax.broadcasted_iota(jnp.int32, (SGU_BLOCK, SGU_BLOCK), 1), CHUNK_SHIFT)
    sgu_mask = pc <= pr
    blocks = []
    for nb in range(TQ // SGU_BLOCK):
        rws = slice(nb * SGU_BLOCK, (nb + 1) * SGU_BLOCK)
        groups = []
        for g in range(GROUPS):
            cols = slice(g * GROUP_DIM, (g + 1) * GROUP_DIM)
            wg = jnp.where(sgu_mask, sgu_w_ref[0, g], 0.0).astype(BF16)
            bias = jnp.broadcast_to(sgu_bt_ref[0, :, g:g + 1], (SGU_BLOCK, GROUP_DIM))
            groups.append(_dot(wg, svn[rws, cols]) + bias)
        blocks.append(jnp.concatenate(groups, axis=1))
    y_c = su * jnp.concatenate(blocks, axis=0)
    merge_c = merge_gate(2)
    diag_values(3, p3)
    merged_s[...] += merge_c * _dot((y_c * gate_c).astype(BF16), w_branch_ref[2])

    strips = [(hd, cs) for hd in range(HEADS) for cs in col_strips]

    def strip_scores(hd, cs, j):
        rows = pl.ds(pl.multiple_of(j * TQ, TQ), TQ)
        return _dot(k_s[hd, rows, :], qt_s[hd, :, cs])

    def strip_update(hd, cs, j, s):
        m_prev = m_s[hd, :, cs]
        m_next = jnp.maximum(m_prev, jnp.max(s, axis=0, keepdims=True))
        alpha = jnp.exp2(m_prev - m_next)
        p = jnp.exp2(s - m_next).astype(BF16)
        acc_s[hd, :, cs] = alpha * acc_s[hd, :, cs] + _dot(values_t(hd, j), p)
        m_s[hd, :, cs] = m_next

    def key_blocks(js):
        pending = []
        for j in js:
            for hd, cs in strips:
                pending.append((hd, cs, j, strip_scores(hd, cs, j)))
                if len(pending) > AHEAD:
                    strip_update(*pending.pop(0))
        for item in pending:
            strip_update(*item)

    def pair_body(jj, c):
        key_blocks((2 * jj, 2 * jj + 1))
        return c

    lax.fori_loop(0, i // 2, pair_body, 0)

    @pl.when(i % 2 == 1)
    def _():
        key_blocks((i - 1,))

    gate_a = _silu_of_half(proj(OFF_GA))
    merge_a = merge_gate(0)
    heads_t = []
    for hd in range(HEADS):
        o = acc_s[hd, 0:V_DIM, :] / acc_s[hd, V_DIM:V_DIM + 1, :]
        o = o[:, :TQ] - lam * o[:, TQ:]
        o = o * lax.rsqrt(jnp.mean(o * o, axis=0, keepdims=True) + EPS) * subln_ref[0]
        heads_t.append(o * (1.0 - lam_init))
    y_a = jnp.concatenate(heads_t, axis=0).T
    merged = merge_a * _dot((y_a * gate_a).astype(BF16), w_branch_ref[0]) + merged_s[...]

    out = _dot(merged.astype(BF16), w_out_ref[...])
    out = out * lax.rsqrt(jnp.mean(out * out, axis=-1, keepdims=True) + EPS) * row(post_g_ref)
    out_ref[0] = x + out


def _full(shape):
    return pl.BlockSpec(shape, lambda b, i: (0,) * len(shape))


def _weight_scales():
    col = np.arange(OFF_GC + WIDTH)
    is_gate = ((col >= OFF_GA) & (col < OFF_P)) | ((col >= OFF_GB) & (col < OFF_SU)) | (col >= OFF_GC)
    return (np.where(is_gate, 0.5, 1.0).astype(np.float32).reshape(1, -1),
            np.full((1, N_BRANCH * D_MODEL), 0.5, np.float32),
            np.ones((1, D_MODEL), np.float32),
            np.full((1, D_MODEL), 0.5, np.float32))


def _layer(x, layer_idx, small, weights, stacked_raw_weights):
    batch, seq, d = x.shape
    assert d == D_MODEL and seq == SEQ and seq % TQ == 0 and TQ % SGU_BLOCK == 0
    lam_init = 0.8 - 0.6 * math.exp(-0.3 * layer_idx)
    w_in, w_merge, w_branch, w_out = weights
    (pre_g, post_g, lq1, lk1, lq2, lk2, subln_g, pool_w, pool_b, pool_scale, ln_g, ln_b, sgu_w,
     sgu_bt, b_merge) = small
    operands = [
        x, pre_g, post_g, w_in, lq1, lk1, lq2, lk2, subln_g, pool_w, pool_b, pool_scale, ln_g,
        ln_b, sgu_w, sgu_bt, w_branch.reshape(N_BRANCH, WIDTH, D_MODEL), w_merge, b_merge, w_out]
    assert len(operands) == N_LAYER_INPUTS
    sliced = {8, 9, 14, 15}
    x_spec = pl.BlockSpec((1, TQ, D_MODEL), lambda b, i: (b, i, 0))

    def of_this_layer(a):
        block = (1,) + a.shape[1:]
        return pl.BlockSpec(block, lambda b, i: (layer_idx,) + (0,) * (len(block) - 1))

    in_specs = [x_spec] + [of_this_layer(a) if n in sliced else _full(a.shape)
                           for n, a in enumerate(operands) if n > 0]
    out_specs = [x_spec]
    out_shape = [jax.ShapeDtypeStruct(x.shape, x.dtype)]
    convert_next = layer_idx + 1 < DEPTH
    if convert_next:
        n_blk = seq // TQ
        n_steps = batch * n_blk
        scales = _weight_scales()
        operands += list(scales) + list(stacked_raw_weights)
        in_specs += [_full(a.shape) for a in scales]
        for w in stacked_raw_weights:
            _, rows, cols = w.shape
            assert rows % (n_steps * 2 * SUBLANES) == 0
            slab = rows // n_steps
            in_specs.append(pl.BlockSpec((1, slab, cols),
                                         lambda b, i: (layer_idx + 1, b * n_blk + i, 0)))
            out_specs.append(pl.BlockSpec((slab, cols), lambda b, i: (b * n_blk + i, 0)))
            out_shape.append(jax.ShapeDtypeStruct((rows, cols), BF16))
    scratch = [
        pltpu.VMEM((HEADS, LANES, 2 * TQ), BF16),
        pltpu.VMEM((HEADS, SEQ, LANES), BF16),
        pltpu.VMEM((SEQ // TQ, HEADS, V_EXT, TQ), BF16),
        pltpu.VMEM((HEADS, 1, 2 * TQ), F32),
        pltpu.VMEM((HEADS, V_EXT, 2 * TQ), F32),
        pltpu.VMEM((TQ + HALO, WIDTH), F32),
        pltpu.VMEM((TQ, D_MODEL), F32),
    ]
    outs = pl.pallas_call(
        functools.partial(_layer_kernel, layer=layer_idx, lam_init=lam_init,
                          convert_next=convert_next),
        grid=(batch, seq // TQ),
        in_specs=in_specs,
        out_specs=out_specs,
        out_shape=out_shape,
        scratch_shapes=scratch,
        compiler_params=pltpu.CompilerParams(
            dimension_semantics=("arbitrary", "arbitrary"),
            vmem_limit_bytes=VMEM_LIMIT_BYTES),
        name=f"hybrid_layer_{layer_idx}",
    )(*operands)
    return outs[0], (tuple(outs[1:]) if convert_next else None)


def kernel(x, pre_norm_g, post_norm_g, w_in, lambda_q1, lambda_k1, lambda_q2, lambda_k2, attn_subln_g, pool_w, pool_b, pool_scale, sgu_ln_g, sgu_ln_b, sgu_w, sgu_b, w_branch, w_merge, b_merge, w_out):
    stacked = (w_in, w_merge, w_branch.reshape(DEPTH, N_BRANCH * WIDTH, D_MODEL), w_out)
    small = (pre_norm_g, post_norm_g, lambda_q1, lambda_k1, lambda_q2, lambda_k2,
             attn_subln_g.reshape(DEPTH, -1, 1), pool_w, pool_b, pool_scale,
             sgu_ln_g, sgu_ln_b, sgu_w, sgu_b.transpose(0, 2, 1), b_merge)
    weights = tuple((w[0] * s).astype(BF16) for w, s in zip(stacked, _weight_scales()))
    for l in range(DEPTH):
        x, weights = _layer(x, l, small, weights, stacked)
    return x
```

```python
import functools
import math

import jax
import jax.numpy as jnp
import numpy as np
from jax import lax
from jax.experimental import pallas as pl
from jax.experimental.pallas import tpu as pltpu

D_MODEL = 1024
SEQ = 2048
DEPTH = 2
CHUNK = 64
CHUNK_SHIFT = CHUNK.bit_length() - 1
WIDTH = D_MODEL // 2
HEADS = 4
QK_DIM = 64
V_DIM = 128
V_EXT = V_DIM + 16
POOL_WINDOWS = (2, 4, 8, 16)
GROUPS = 4
GROUP_DIM = WIDTH // GROUPS
SGU_BLOCK = 128
N_BRANCH = 3
EPS = 1e-6
LANES = 128
SUBLANES = 8
HALO = SUBLANES * (max(POOL_WINDOWS).bit_length() - 1)
TQ = 512
COL_STRIP = 256
AHEAD = 8
STAGE_ROWS = 64
V7X_VMEM_BYTES = 64 * 1024 * 1024
VMEM_LIMIT_BYTES = V7X_VMEM_BYTES - V7X_VMEM_BYTES // 16

OFF_Q, OFF_K, OFF_V, OFF_GA, OFF_P, OFF_GB, OFF_SU, OFF_SV, OFF_GC = (
    0, 512, 1024, 1536, 2048, 2560, 3072, 3584, 4096)
IN_WIDTH = OFF_GC + WIDTH

F32 = jnp.float32
BF16 = jnp.bfloat16


def _dot(a, b):
    return jnp.dot(a, b, preferred_element_type=F32)


def _silu_of_half(xh):
    return xh * (1.0 + jnp.tanh(xh))


def _twice_sigmoid_of_half(xh):
    return 1.0 + jnp.tanh(xh)


def _gelu(x):
    return 0.5 * x * (1.0 + lax.erf(x * math.sqrt(0.5)))


def _layer_kernel(x_ref, pre_g_ref, post_g_ref, lq1_ref, lk1_ref, lq2_ref, lk2_ref, subln_ref,
                  pool_w_ref, pool_b_ref, pool_scale_ref, ln_g_ref, ln_b_ref, sgu_w_ref,
                  sgu_bt_ref, b_merge_ref, in_scale_ref, w_in_hbm, w_merge_hbm, w_branch_hbm,
                  w_out_hbm, out_ref,
                  w_in_s, w_merge_s, w_branch_s, w_out_s, stage_s, stage_sem,
                  qt_s, k_s, vt_s, m_s, acc_s, ext_s, merged_s, *, layer, lam_init):
    i = pl.program_id(1)

    def load_weight(src_hbm, dst_s, scale):
        rows, width = dst_s.shape
        n_chunks = rows // STAGE_ROWS

        def chunk_copy(c, slot):
            return pltpu.make_async_copy(
                src_hbm.at[layer, pl.ds(c * STAGE_ROWS, STAGE_ROWS), :],
                stage_s.at[slot, :, pl.ds(0, width)], stage_sem.at[slot])

        chunk_copy(0, 0).start()

        def body(c, carry):
            slot = lax.rem(c, 2)

            @pl.when(c + 1 < n_chunks)
            def _():
                chunk_copy(c + 1, 1 - slot).start()

            chunk_copy(c, slot).wait()
            dst_rows = pl.ds(pl.multiple_of(c * STAGE_ROWS, STAGE_ROWS), STAGE_ROWS)
            dst_s[dst_rows, :] = (stage_s[slot, :, 0:width] * scale).astype(BF16)
            return carry

        lax.fori_loop(0, n_chunks, body, 0)

    @pl.when((pl.program_id(0) == 0) & (i == 0))
    def _():
        load_weight(w_in_hbm, w_in_s, in_scale_ref[...])
        load_weight(w_merge_hbm, w_merge_s, 0.5)
        load_weight(w_branch_hbm, w_branch_s, 1.0)
        load_weight(w_out_hbm, w_out_s, 0.5)

    def row(ref):
        return ref[layer:layer + 1, :]

    def branch_w(n):
        return w_branch_s[n * WIDTH:(n + 1) * WIDTH, :]

    @pl.when(i == 0)
    def _():
        ext_s[0:HALO, :] = jnp.zeros((HALO, WIDTH), F32)

    x = x_ref[0]

    h = x * lax.rsqrt(jnp.mean(x * x, axis=-1, keepdims=True) + EPS) * row(pre_g_ref)
    hb = h.astype(BF16)

    def proj(off, width=WIDTH):
        return _dot(hb, w_in_s[:, off:off + width])

    def merge_gate(n):
        cols = slice(n * D_MODEL, (n + 1) * D_MODEL)
        logits = _dot(hb, w_merge_s[:, cols]) + 0.5 * b_merge_ref[layer:layer + 1, cols]
        return _twice_sigmoid_of_half(logits)

    lam = (jnp.exp(jnp.sum(row(lq1_ref) * row(lk1_ref), axis=-1, keepdims=True))
           - jnp.exp(jnp.sum(row(lq2_ref) * row(lk2_ref), axis=-1, keepdims=True))
           + lam_init)

    rows_i = pl.ds(pl.multiple_of(i * TQ, TQ), TQ)
    q_t = (proj(OFF_Q) * (QK_DIM ** -0.5 * math.log2(math.e))).T
    k = proj(OFF_K).astype(BF16)
    v_t = proj(OFF_V).T
    ones = jnp.ones((V_EXT - V_DIM, TQ), BF16)
    for hd in range(HEADS):
        vt_s[i, hd, 0:V_DIM, :] = v_t[hd * V_DIM:(hd + 1) * V_DIM].astype(BF16)
        vt_s[i, hd, V_DIM:V_EXT, :] = ones
    first_map = lax.broadcasted_iota(jnp.int32, (LANES, TQ), 0) < QK_DIM
    zero = jnp.zeros((LANES, TQ), BF16)
    for hd in range(HEADS):
        cols = slice(hd * LANES, (hd + 1) * LANES)
        q_hd = q_t[cols].astype(BF16)
        qt_s[hd, :, 0:TQ] = jnp.where(first_map, q_hd, zero)
        qt_s[hd, :, TQ:2 * TQ] = jnp.where(first_map, zero, q_hd)
        k_s[hd, rows_i, :] = k[:, cols]

    def values_t(hd, j):
        return vt_s[j, hd]

    col_strips = [slice(c, c + COL_STRIP) for c in range(0, 2 * TQ, COL_STRIP)]

    def visible_keys(cs):
        return cs.start % TQ + COL_STRIP

    def chunk_of(shape, axis, offset=0):
        return lax.shift_right_logical(lax.broadcasted_iota(jnp.int32, shape, axis) + offset,
                                       CHUNK_SHIFT)

    def diag_probs(hd):
        probs = []
        for cs in col_strips:
            nk = visible_keys(cs)
            keys = k_s[hd, pl.ds(pl.multiple_of(i * TQ, TQ), nk), :]
            s = _dot(keys, qt_s[hd, :, cs])
            visible = chunk_of(s.shape, 0) <= chunk_of(s.shape, 1, cs.start % TQ)
            s = jnp.where(visible, s, -jnp.inf)
            mx = jnp.max(s, axis=0, keepdims=True)
            m_s[hd, :, cs] = mx
            probs.append(jnp.exp2(s - mx).astype(BF16))
        return probs

    def diag_values(hd, probs):
        for cs, p in zip(col_strips, probs):
            acc_s[hd, :, cs] = _dot(vt_s[i, hd, :, 0:visible_keys(cs)], p)

    p0 = diag_probs(0)

    su = _gelu(proj(OFF_SU))
    p1 = diag_probs(1)
    sv = _gelu(proj(OFF_SV))
    diag_values(0, p0)
    mu = jnp.mean(sv, axis=-1, keepdims=True)
    cen = sv - mu
    var = jnp.mean(cen * cen, axis=-1, keepdims=True)
    svn = (cen * lax.rsqrt(var + EPS) * row(ln_g_ref) + row(ln_b_ref)).astype(BF16)
    p2 = diag_probs(2)

    ext_s[HALO:HALO + TQ, :] = proj(OFF_P)
    gate_b = _silu_of_half(proj(OFF_GB))
    diag_values(1, p1)
    p3 = diag_probs(3)
    pos = lax.broadcasted_iota(jnp.int32, (TQ, LANES), 0) + (i * TQ + 1)
    mixed = []
    for g, win in enumerate(POOL_WINDOWS):
        cols = slice(g * GROUP_DIM, (g + 1) * GROUP_DIM)
        wsum = ext_s[:, cols]
        shift = 1
        while shift < win:
            rows = wsum.shape[0] - SUBLANES
            wsum = wsum[SUBLANES:] + wsum[SUBLANES - shift:SUBLANES - shift + rows]
            shift *= 2
        wsum = wsum[wsum.shape[0] - TQ:]
        u = ext_s[HALO:HALO + TQ, cols]
        count = jnp.minimum(pos, win).astype(F32)
        pooled = (wsum / count - u).astype(BF16)
        mixed.append(_dot(pooled, pool_w_ref[0, g].astype(BF16)) + pool_b_ref[layer, g:g + 1, :])
    y_b = jnp.concatenate(mixed, axis=1) * row(pool_scale_ref)
    ext_s[0:HALO, :] = ext_s[TQ:TQ + HALO, :]
    merge_b = merge_gate(1)
    merged_s[...] = merge_b * _dot((y_b * gate_b).astype(BF16), branch_w(1))

    gate_c = _silu_of_half(proj(OFF_GC))
    diag_values(2, p2)
    sgu_shape = (SGU_BLOCK, SGU_BLOCK)
    sgu_mask = chunk_of(sgu_shape, 1) <= chunk_of(sgu_shape, 0)
    blocks = []
    for nb in range(TQ // SGU_BLOCK):
        rws = slice(nb * SGU_BLOCK, (nb + 1) * SGU_BLOCK)
        groups = []
        for g in range(GROUPS):
            cols = slice(g * GROUP_DIM, (g + 1) * GROUP_DIM)
            wg = jnp.where(sgu_mask, sgu_w_ref[0, g], 0.0).astype(BF16)
            bias = jnp.broadcast_to(sgu_bt_ref[0, :, g:g + 1], (SGU_BLOCK, GROUP_DIM))
            groups.append(_dot(wg, svn[rws, cols]) + bias)
        blocks.append(jnp.concatenate(groups, axis=1))
    y_c = su * jnp.concatenate(blocks, axis=0)
    merge_c = merge_gate(2)
    diag_values(3, p3)
    merged_s[...] += merge_c * _dot((y_c * gate_c).astype(BF16), branch_w(2))

    strips = [(hd, cs) for hd in range(HEADS) for cs in col_strips]

    def strip_scores(hd, cs, j):
        rows = pl.ds(pl.multiple_of(j * TQ, TQ), TQ)
        return _dot(k_s[hd, rows, :], qt_s[hd, :, cs])

    def strip_update(hd, cs, j, s):
        m_prev = m_s[hd, :, cs]
        m_next = jnp.maximum(m_prev, jnp.max(s, axis=0, keepdims=True))
        alpha = jnp.exp2(m_prev - m_next)
        p = jnp.exp2(s - m_next).astype(BF16)
        acc_s[hd, :, cs] = alpha * acc_s[hd, :, cs] + _dot(values_t(hd, j), p)
        m_s[hd, :, cs] = m_next

    def key_blocks(js):
        pending = []
        for j in js:
            for hd, cs in strips:
                pending.append((hd, cs, j, strip_scores(hd, cs, j)))
                if len(pending) > AHEAD:
                    strip_update(*pending.pop(0))
        for item in pending:
            strip_update(*item)

    def pair_body(jj, c):
        key_blocks((2 * jj, 2 * jj + 1))
        return c

    lax.fori_loop(0, i // 2, pair_body, 0)

    @pl.when(i % 2 == 1)
    def _():
        key_blocks((i - 1,))

    gate_a = _silu_of_half(proj(OFF_GA))
    merge_a = merge_gate(0)
    heads_t = []
    for hd in range(HEADS):
        o = acc_s[hd, 0:V_DIM, :] / acc_s[hd, V_DIM:V_DIM + 1, :]
        o = o[:, :TQ] - lam * o[:, TQ:]
        o = o * lax.rsqrt(jnp.mean(o * o, axis=0, keepdims=True) + EPS) * subln_ref[0]
        heads_t.append(o * (1.0 - lam_init))
    y_a = jnp.concatenate(heads_t, axis=0).T
    merged = merge_a * _dot((y_a * gate_a).astype(BF16), branch_w(0)) + merged_s[...]

    out = _dot(merged.astype(BF16), w_out_s[...])
    out = out * lax.rsqrt(jnp.mean(out * out, axis=-1, keepdims=True) + EPS) * row(post_g_ref)
    out_ref[0] = x + out


def _whole(a):
    return pl.BlockSpec(a.shape, lambda b, i: (0,) * a.ndim)


def _layer(x, layer_idx, small, big):
    batch, seq, d = x.shape
    assert d == D_MODEL and seq == SEQ and seq % TQ == 0 and TQ % SGU_BLOCK == 0
    lam_init = 0.8 - 0.6 * math.exp(-0.3 * layer_idx)
    (pre_g, post_g, lq1, lk1, lq2, lk2, subln_g, pool_w, pool_b, pool_scale, ln_g, ln_b, sgu_w,
     sgu_bt, b_merge) = small
    col = np.arange(IN_WIDTH)
    is_gate = ((col >= OFF_GA) & (col < OFF_P)) | ((col >= OFF_GB) & (col < OFF_SU)) | (col >= OFF_GC)
    in_scale = np.where(is_gate, 0.5, 1.0).astype(np.float32).reshape(1, -1)

    x_spec = pl.BlockSpec((1, TQ, D_MODEL), lambda b, i: (b, i, 0))

    def of_this_layer(a):
        block = (1,) + a.shape[1:]
        return pl.BlockSpec(block, lambda b, i: (layer_idx,) + (0,) * (len(block) - 1))

    operands_and_specs = [
        (x, x_spec), (pre_g, _whole(pre_g)), (post_g, _whole(post_g)), (lq1, _whole(lq1)),
        (lk1, _whole(lk1)), (lq2, _whole(lq2)), (lk2, _whole(lk2)),
        (subln_g, of_this_layer(subln_g)), (pool_w, of_this_layer(pool_w)),
        (pool_b, _whole(pool_b)), (pool_scale, _whole(pool_scale)), (ln_g, _whole(ln_g)),
        (ln_b, _whole(ln_b)), (sgu_w, of_this_layer(sgu_w)), (sgu_bt, of_this_layer(sgu_bt)),
        (b_merge, _whole(b_merge)), (in_scale, _whole(in_scale)),
    ] + [(w, pl.BlockSpec(memory_space=pl.ANY)) for w in big]
    for w in big:
        assert w.shape[1] % STAGE_ROWS == 0 and w.shape[2] <= IN_WIDTH
    scratch = [pltpu.VMEM(w.shape[1:], BF16) for w in big] + [
        pltpu.VMEM((2, STAGE_ROWS, IN_WIDTH), F32),
        pltpu.SemaphoreType.DMA((2,)),
        pltpu.VMEM((HEADS, LANES, 2 * TQ), BF16),
        pltpu.VMEM((HEADS, SEQ, LANES), BF16),
        pltpu.VMEM((SEQ // TQ, HEADS, V_EXT, TQ), BF16),
        pltpu.VMEM((HEADS, 1, 2 * TQ), F32),
        pltpu.VMEM((HEADS, V_EXT, 2 * TQ), F32),
        pltpu.VMEM((TQ + HALO, WIDTH), F32),
        pltpu.VMEM((TQ, D_MODEL), F32),
    ]
    return pl.pallas_call(
        functools.partial(_layer_kernel, layer=layer_idx, lam_init=lam_init),
        grid=(batch, seq // TQ),
        in_specs=[spec for _, spec in operands_and_specs],
        out_specs=x_spec,
        out_shape=jax.ShapeDtypeStruct(x.shape, x.dtype),
        scratch_shapes=scratch,
        compiler_params=pltpu.CompilerParams(
            dimension_semantics=("arbitrary", "arbitrary"),
            vmem_limit_bytes=VMEM_LIMIT_BYTES),
        name=f"hybrid_layer_{layer_idx}",
    )(*[a for a, _ in operands_and_specs])


def kernel(x, pre_norm_g, post_norm_g, w_in, lambda_q1, lambda_k1, lambda_q2, lambda_k2, attn_subln_g, pool_w, pool_b, pool_scale, sgu_ln_g, sgu_ln_b, sgu_w, sgu_b, w_branch, w_merge, b_merge, w_out):
    big = (w_in, w_merge, w_branch.reshape(DEPTH, N_BRANCH * WIDTH, D_MODEL), w_out)
    small = (pre_norm_g, post_norm_g, lambda_q1, lambda_k1, lambda_q2, lambda_k2,
             attn_subln_g.reshape(DEPTH, -1, 1), pool_w, pool_b, pool_scale,
             sgu_ln_g, sgu_ln_b, sgu_w, sgu_b.transpose(0, 2, 1), b_merge)
    for l in range(DEPTH):
        x = _layer(x, l, small, big)
    return x
```

```python
import functools
import math

import jax
import jax.numpy as jnp
import numpy as np
from jax import lax
from jax.experimental import pallas as pl
from jax.experimental.pallas import tpu as pltpu

D_MODEL = 1024
SEQ = 2048
DEPTH = 2
CHUNK = 64
CHUNK_SHIFT = CHUNK.bit_length() - 1
WIDTH = D_MODEL // 2
HEADS = 4
QK_DIM = 64
V_DIM = 128
V_EXT = V_DIM + 16
POOL_WINDOWS = (2, 4, 8, 16)
GROUPS = 4
GROUP_DIM = WIDTH // GROUPS
SGU_BLOCK = 128
N_BRANCH = 3
EPS = 1e-6
LANES = 128
SUBLANES = 8
HALO = SUBLANES * (max(POOL_WINDOWS).bit_length() - 1)
TQ = 512
COL_STRIP = 256
AHEAD = 16
KEY_SUB = 256
V7X_VMEM_BYTES = 64 * 1024 * 1024
VMEM_LIMIT_BYTES = V7X_VMEM_BYTES - V7X_VMEM_BYTES // 16

OFF_Q, OFF_K, OFF_V, OFF_GA, OFF_P, OFF_GB, OFF_SU, OFF_SV, OFF_GC = (
    0, 512, 1024, 1536, 2048, 2560, 3072, 3584, 4096)

F32 = jnp.float32
BF16 = jnp.bfloat16


def _dot(a, b):
    return jnp.dot(a, b, preferred_element_type=F32)


def _silu_of_half(xh):
    return xh * (1.0 + jnp.tanh(xh))


def _twice_sigmoid_of_half(xh):
    return 1.0 + jnp.tanh(xh)


def _gelu(x):
    return 0.5 * x * (1.0 + lax.erf(x * math.sqrt(0.5)))


N_LAYER_INPUTS = 20
N_NEXT_WEIGHTS = 4


def _layer_kernel(*refs, layer, lam_init, convert_next):
    (x_ref, pre_g_ref, post_g_ref, w_in_ref, lq1_ref, lk1_ref, lq2_ref, lk2_ref, subln_ref,
     pool_w_ref, pool_b_ref, pool_scale_ref, ln_g_ref, ln_b_ref, sgu_w_ref, sgu_bt_ref,
     w_branch_ref, w_merge_ref, b_merge_ref, w_out_ref) = refs[:N_LAYER_INPUTS]
    refs = refs[N_LAYER_INPUTS:]
    if convert_next:
        scales, raw, refs = refs[:N_NEXT_WEIGHTS], refs[N_NEXT_WEIGHTS:2 * N_NEXT_WEIGHTS], \
            refs[2 * N_NEXT_WEIGHTS:]
        out_ref, cast, refs = refs[0], refs[1:1 + N_NEXT_WEIGHTS], refs[1 + N_NEXT_WEIGHTS:]
        for scale_ref, raw_ref, cast_ref in zip(scales, raw, cast):
            cast_ref[...] = (raw_ref[0] * scale_ref[...]).astype(BF16)
    else:
        out_ref, refs = refs[0], refs[1:]
    qt_s, k_s, vt_s, m_s, acc_s, ext_s, merged_s = refs

    def row(ref):
        return ref[layer:layer + 1, :]

    i = pl.program_id(1)

    @pl.when(i == 0)
    def _():
        ext_s[0:HALO, :] = jnp.zeros((HALO, WIDTH), F32)

    x = x_ref[0]

    h = x * lax.rsqrt(jnp.mean(x * x, axis=-1, keepdims=True) + EPS) * row(pre_g_ref)
    hb = h.astype(BF16)

    def proj(off, width=WIDTH):
        return _dot(hb, w_in_ref[:, off:off + width])

    def merge_gate(n):
        cols = slice(n * D_MODEL, (n + 1) * D_MODEL)
        logits = _dot(hb, w_merge_ref[:, cols]) + 0.5 * b_merge_ref[layer:layer + 1, cols]
        return _twice_sigmoid_of_half(logits)

    lam = (jnp.exp(jnp.sum(row(lq1_ref) * row(lk1_ref), axis=-1, keepdims=True))
           - jnp.exp(jnp.sum(row(lq2_ref) * row(lk2_ref), axis=-1, keepdims=True))
           + lam_init)

    rows_i = pl.ds(pl.multiple_of(i * TQ, TQ), TQ)
    q_t = (proj(OFF_Q) * (QK_DIM ** -0.5 * math.log2(math.e))).T
    k = proj(OFF_K).astype(BF16)
    v_t = proj(OFF_V).T
    ones = jnp.ones((V_EXT - V_DIM, TQ), BF16)
    for hd in range(HEADS):
        vt_s[i, hd, 0:V_DIM, :] = v_t[hd * V_DIM:(hd + 1) * V_DIM].astype(BF16)
        vt_s[i, hd, V_DIM:V_EXT, :] = ones
    first_map = lax.broadcasted_iota(jnp.int32, (LANES, TQ), 0) < QK_DIM
    zero = jnp.zeros((LANES, TQ), BF16)
    for hd in range(HEADS):
        cols = slice(hd * LANES, (hd + 1) * LANES)
        q_hd = q_t[cols].astype(BF16)
        qt_s[hd, :, 0:TQ] = jnp.where(first_map, q_hd, zero)
        qt_s[hd, :, TQ:2 * TQ] = jnp.where(first_map, zero, q_hd)
        k_s[hd, rows_i, :] = k[:, cols]

    def values_t(hd, j):
        return vt_s[j, hd]

    col_strips = [slice(c, c + COL_STRIP) for c in range(0, 2 * TQ, COL_STRIP)]

    def visible_keys(cs):
        return cs.start % TQ + COL_STRIP

    def diag_probs(hd):
        probs = []
        for cs in col_strips:
            nk = visible_keys(cs)
            keys = k_s[hd, pl.ds(pl.multiple_of(i * TQ, TQ), nk), :]
            s = _dot(keys, qt_s[hd, :, cs])
            key_chunk = lax.shift_right_logical(
                lax.broadcasted_iota(jnp.int32, (nk, COL_STRIP), 0), CHUNK_SHIFT)
            qry_chunk = lax.shift_right_logical(
                lax.broadcasted_iota(jnp.int32, (nk, COL_STRIP), 1) + cs.start % TQ, CHUNK_SHIFT)
            s = jnp.where(key_chunk <= qry_chunk, s, -jnp.inf)
            mx = jnp.max(s, axis=0, keepdims=True)
            m_s[hd, :, cs] = mx
            probs.append(jnp.exp2(s - mx).astype(BF16))
        return probs

    def diag_values(hd, probs):
        for cs, p in zip(col_strips, probs):
            acc_s[hd, :, cs] = _dot(vt_s[i, hd, :, 0:visible_keys(cs)], p)

    p0 = diag_probs(0)

    su = _gelu(proj(OFF_SU))
    p1 = diag_probs(1)
    sv = _gelu(proj(OFF_SV))
    diag_values(0, p0)
    mu = jnp.mean(sv, axis=-1, keepdims=True)
    cen = sv - mu
    var = jnp.mean(cen * cen, axis=-1, keepdims=True)
    svn = (cen * lax.rsqrt(var + EPS) * row(ln_g_ref) + row(ln_b_ref)).astype(BF16)
    p2 = diag_probs(2)

    ext_s[HALO:HALO + TQ, :] = proj(OFF_P)
    gate_b = _silu_of_half(proj(OFF_GB))
    diag_values(1, p1)
    p3 = diag_probs(3)
    pos = lax.broadcasted_iota(jnp.int32, (TQ, LANES), 0) + (i * TQ + 1)
    mixed = []
    for g, win in enumerate(POOL_WINDOWS):
        cols = slice(g * GROUP_DIM, (g + 1) * GROUP_DIM)
        wsum = ext_s[:, cols]
        shift = 1
        while shift < win:
            rows = wsum.shape[0] - SUBLANES
            wsum = wsum[SUBLANES:] + wsum[SUBLANES - shift:SUBLANES - shift + rows]
            shift *= 2
        wsum = wsum[wsum.shape[0] - TQ:]
        u = ext_s[HALO:HALO + TQ, cols]
        count = jnp.minimum(pos, win).astype(F32)
        pooled = (wsum / count - u).astype(BF16)
        mixed.append(_dot(pooled, pool_w_ref[0, g].astype(BF16)) + pool_b_ref[layer, g:g + 1, :])
    y_b = jnp.concatenate(mixed, axis=1) * row(pool_scale_ref)
    ext_s[0:HALO, :] = ext_s[TQ:TQ + HALO, :]
    merge_b = merge_gate(1)
    merged_s[...] = merge_b * _dot((y_b * gate_b).astype(BF16), w_branch_ref[1])

    gate_c = _silu_of_half(proj(OFF_GC))
    diag_values(2, p2)
    pr = lax.shift_right_logical(lax.broadcasted_iota(jnp.int32, (SGU_BLOCK, SGU_BLOCK), 0), CHUNK_SHIFT)
    pc = lax.shift_right_logical(lax.broadcasted_iota(jnp.int32, (SGU_BLOCK, SGU_BLOCK), 1), CHUNK_SHIFT)
    sgu_mask = pc <= pr
    blocks = []
    for nb in range(TQ // SGU_BLOCK):
        rws = slice(nb * SGU_BLOCK, (nb + 1) * SGU_BLOCK)
        groups = []
        for g in range(GROUPS):
            cols = slice(g * GROUP_DIM, (g + 1) * GROUP_DIM)
            wg = jnp.where(sgu_mask, sgu_w_ref[0, g], 0.0).astype(BF16)
            bias = jnp.broadcast_to(sgu_bt_ref[0, :, g:g + 1], (SGU_BLOCK, GROUP_DIM))
            groups.append(_dot(wg, svn[rws, cols]) + bias)
        blocks.append(jnp.concatenate(groups, axis=1))
    y_c = su * jnp.concatenate(blocks, axis=0)
    merge_c = merge_gate(2)
    diag_values(3, p3)
    merged_s[...] += merge_c * _dot((y_c * gate_c).astype(BF16), w_branch_ref[2])

    strips = [(hd, cs) for hd in range(HEADS) for cs in col_strips]

    def strip_scores(hd, cs, j, kh):
        rows = pl.ds(pl.multiple_of(j * TQ + kh * KEY_SUB, KEY_SUB), KEY_SUB)
        return _dot(k_s[hd, rows, :], qt_s[hd, :, cs])

    def strip_update(hd, cs, j, kh, s):
        m_prev = m_s[hd, :, cs]
        m_next = jnp.maximum(m_prev, jnp.max(s, axis=0, keepdims=True))
        alpha = jnp.exp2(m_prev - m_next)
        p = jnp.exp2(s - m_next).astype(BF16)
        vals = vt_s[j, hd, :, kh * KEY_SUB:(kh + 1) * KEY_SUB]
        acc_s[hd, :, cs] = alpha * acc_s[hd, :, cs] + _dot(vals, p)
        m_s[hd, :, cs] = m_next

    def key_blocks(js):
        pending = []
        for j in js:
            for hd, cs in strips:
                for kh in range(TQ // KEY_SUB):
                    pending.append((hd, cs, j, kh, strip_scores(hd, cs, j, kh)))
                    if len(pending) > AHEAD:
                        strip_update(*pending.pop(0))
        for item in pending:
            strip_update(*item)

    def pair_body(jj, c):
        key_blocks((2 * jj, 2 * jj + 1))
        return c

    lax.fori_loop(0, i // 2, pair_body, 0)

    @pl.when(i % 2 == 1)
    def _():
        key_blocks((i - 1,))

    gate_a = _silu_of_half(proj(OFF_GA))
    merge_a = merge_gate(0)
    heads_t = []
    for hd in range(HEADS):
        o = acc_s[hd, 0:V_DIM, :] / acc_s[hd, V_DIM:V_DIM + 1, :]
        o = o[:, :TQ] - lam * o[:, TQ:]
        o = o * lax.rsqrt(jnp.mean(o * o, axis=0, keepdims=True) + EPS) * subln_ref[0]
        heads_t.append(o * (1.0 - lam_init))
    y_a = jnp.concatenate(heads_t, axis=0).T
    merged = merge_a * _dot((y_a * gate_a).astype(BF16), w_branch_ref[0]) + merged_s[...]

    out = _dot(merged.astype(BF16), w_out_ref[...])
    out = out * lax.rsqrt(jnp.mean(out * out, axis=-1, keepdims=True) + EPS) * row(post_g_ref)
    out_ref[0] = x + out


def _full(shape):
    return pl.BlockSpec(shape, lambda b, i: (0,) * len(shape))


def _weight_scales():
    col = np.arange(OFF_GC + WIDTH)
    is_gate = ((col >= OFF_GA) & (col < OFF_P)) | ((col >= OFF_GB) & (col < OFF_SU)) | (col >= OFF_GC)
    return (np.where(is_gate, 0.5, 1.0).astype(np.float32).reshape(1, -1),
            np.full((1, N_BRANCH * D_MODEL), 0.5, np.float32),
            np.ones((1, D_MODEL), np.float32),
            np.full((1, D_MODEL), 0.5, np.float32))


def _layer(x, layer_idx, small, weights, stacked_raw_weights):
    batch, seq, d = x.shape
    assert d == D_MODEL and seq == SEQ and seq % TQ == 0 and TQ % SGU_BLOCK == 0
    lam_init = 0.8 - 0.6 * math.exp(-0.3 * layer_idx)
    w_in, w_merge, w_branch, w_out = weights
    (pre_g, post_g, lq1, lk1, lq2, lk2, subln_g, pool_w, pool_b, pool_scale, ln_g, ln_b, sgu_w,
     sgu_bt, b_merge) = small
    operands = [
        x, pre_g, post_g, w_in, lq1, lk1, lq2, lk2, subln_g, pool_w, pool_b, pool_scale, ln_g,
        ln_b, sgu_w, sgu_bt, w_branch.reshape(N_BRANCH, WIDTH, D_MODEL), w_merge, b_merge, w_out]
    assert len(operands) == N_LAYER_INPUTS
    sliced = {8, 9, 14, 15}
    x_spec = pl.BlockSpec((1, TQ, D_MODEL), lambda b, i: (b, i, 0))

    def of_this_layer(a):
        block = (1,) + a.shape[1:]
        return pl.BlockSpec(block, lambda b, i: (layer_idx,) + (0,) * (len(block) - 1))

    in_specs = [x_spec] + [of_this_layer(a) if n in sliced else _full(a.shape)
                           for n, a in enumerate(operands) if n > 0]
    out_specs = [x_spec]
    out_shape = [jax.ShapeDtypeStruct(x.shape, x.dtype)]
    convert_next = layer_idx + 1 < DEPTH
    if convert_next:
        n_blk = seq // TQ
        n_steps = batch * n_blk
        scales = _weight_scales()
        operands += list(scales) + list(stacked_raw_weights)
        in_specs += [_full(a.shape) for a in scales]
        for w in stacked_raw_weights:
            _, rows, cols = w.shape
            assert rows % (n_steps * 2 * SUBLANES) == 0
            slab = rows // n_steps
            in_specs.append(pl.BlockSpec((1, slab, cols),
                                         lambda b, i: (layer_idx + 1, b * n_blk + i, 0)))
            out_specs.append(pl.BlockSpec((slab, cols), lambda b, i: (b * n_blk + i, 0)))
            out_shape.append(jax.ShapeDtypeStruct((rows, cols), BF16))
    scratch = [
        pltpu.VMEM((HEADS, LANES, 2 * TQ), BF16),
        pltpu.VMEM((HEADS, SEQ, LANES), BF16),
        pltpu.VMEM((SEQ // TQ, HEADS, V_EXT, TQ), BF16),
        pltpu.VMEM((HEADS, 1, 2 * TQ), F32),
        pltpu.VMEM((HEADS, V_EXT, 2 * TQ), F32),
        pltpu.VMEM((TQ + HALO, WIDTH), F32),
        pltpu.VMEM((TQ, D_MODEL), F32),
    ]
    outs = pl.pallas_call(
        functools.partial(_layer_kernel, layer=layer_idx, lam_init=lam_init,
                          convert_next=convert_next),
        grid=(batch, seq // TQ),
        in_specs=in_specs,
        out_specs=out_specs,
        out_shape=out_shape,
        scratch_shapes=scratch,
        compiler_params=pltpu.CompilerParams(
            dimension_semantics=("arbitrary", "arbitrary"),
            vmem_limit_bytes=VMEM_LIMIT_BYTES),
        name=f"hybrid_layer_{layer_idx}",
    )(*operands)
    return outs[0], (tuple(outs[1:]) if convert_next else None)


def kernel(x, pre_norm_g, post_norm_g, w_in, lambda_q1, lambda_k1, lambda_q2, lambda_k2, attn_subln_g, pool_w, pool_b, pool_scale, sgu_ln_g, sgu_ln_b, sgu_w, sgu_b, w_branch, w_merge, b_merge, w_out):
    stacked = (w_in, w_merge, w_branch.reshape(DEPTH, N_BRANCH * WIDTH, D_MODEL), w_out)
    small = (pre_norm_g, post_norm_g, lambda_q1, lambda_k1, lambda_q2, lambda_k2,
             attn_subln_g.reshape(DEPTH, -1, 1), pool_w, pool_b, pool_scale,
             sgu_ln_g, sgu_ln_b, sgu_w, sgu_b.transpose(0, 2, 1), b_merge)
    weights = tuple((w[0] * s).astype(BF16) for w, s in zip(stacked, _weight_scales()))
    for l in range(DEPTH):
        x, weights = _layer(x, l, small, weights, stacked)
    return x
```

```python
import functools
import math

import jax
import jax.numpy as jnp
import numpy as np
from jax import lax
from jax.experimental import pallas as pl
from jax.experimental.pallas import tpu as pltpu

D_MODEL = 1024
SEQ = 2048
DEPTH = 2
CHUNK = 64
CHUNK_SHIFT = CHUNK.bit_length() - 1
WIDTH = D_MODEL // 2
HEADS = 4
QK_DIM = 64
V_DIM = 128
V_EXT = V_DIM + 16
POOL_WINDOWS = (2, 4, 8, 16)
GROUPS = 4
GROUP_DIM = WIDTH // GROUPS
SGU_BLOCK = 128
N_BRANCH = 3
EPS = 1e-6
LANES = 128
SUBLANES = 8
HALO = SUBLANES * (max(POOL_WINDOWS).bit_length() - 1)
TQ = 512
COL_STRIP = 256
AHEAD = 16
KEY_SUB = 256
V7X_VMEM_BYTES = 64 * 1024 * 1024
VMEM_LIMIT_BYTES = V7X_VMEM_BYTES - V7X_VMEM_BYTES // 16

OFF_Q, OFF_K, OFF_V, OFF_GA, OFF_P, OFF_GB, OFF_SU, OFF_SV, OFF_GC = (
    0, 512, 1024, 1536, 2048, 2560, 3072, 3584, 4096)

F32 = jnp.float32
BF16 = jnp.bfloat16


def _dot(a, b):
    return jnp.dot(a, b, preferred_element_type=F32)


def _silu_of_half(xh):
    return xh * (1.0 + jnp.tanh(xh))


def _twice_sigmoid_of_half(xh):
    return 1.0 + jnp.tanh(xh)


def _gelu(x):
    return 0.5 * x * (1.0 + lax.erf(x * math.sqrt(0.5)))


N_LAYER_INPUTS = 20
N_NEXT_WEIGHTS = 4


def _layer_kernel(*refs, layer, lam_init, convert_next):
    (x_ref, pre_g_ref, post_g_ref, w_in_ref, lq1_ref, lk1_ref, lq2_ref, lk2_ref, subln_ref,
     pool_w_ref, pool_b_ref, pool_scale_ref, ln_g_ref, ln_b_ref, sgu_w_ref, sgu_bt_ref,
     w_branch_ref, w_merge_ref, b_merge_ref, w_out_ref) = refs[:N_LAYER_INPUTS]
    refs = refs[N_LAYER_INPUTS:]
    if convert_next:
        scales, raw, refs = refs[:N_NEXT_WEIGHTS], refs[N_NEXT_WEIGHTS:2 * N_NEXT_WEIGHTS], \
            refs[2 * N_NEXT_WEIGHTS:]
        out_ref, cast, refs = refs[0], refs[1:1 + N_NEXT_WEIGHTS], refs[1 + N_NEXT_WEIGHTS:]
        for scale_ref, raw_ref, cast_ref in zip(scales, raw, cast):
            cast_ref[...] = (raw_ref[0] * scale_ref[...]).astype(BF16)
    else:
        out_ref, refs = refs[0], refs[1:]
    qt_s, k_s, vt_s, m_s, acc_s, ext_s, merged_s = refs

    def row(ref):
        return ref[layer:layer + 1, :]

    i = pl.program_id(1)

    @pl.when(i == 0)
    def _():
        ext_s[0:HALO, :] = jnp.zeros((HALO, WIDTH), F32)

    x = x_ref[0]

    h = x * lax.rsqrt(jnp.mean(x * x, axis=-1, keepdims=True) + EPS) * row(pre_g_ref)
    hb = h.astype(BF16)

    def proj(off, width=WIDTH):
        return _dot(hb, w_in_ref[:, off:off + width])

    def merge_gate(n):
        cols = slice(n * D_MODEL, (n + 1) * D_MODEL)
        logits = _dot(hb, w_merge_ref[:, cols]) + 0.5 * b_merge_ref[layer:layer + 1, cols]
        return _twice_sigmoid_of_half(logits)

    lam = (jnp.exp(jnp.sum(row(lq1_ref) * row(lk1_ref), axis=-1, keepdims=True))
           - jnp.exp(jnp.sum(row(lq2_ref) * row(lk2_ref), axis=-1, keepdims=True))
           + lam_init)

    rows_i = pl.ds(pl.multiple_of(i * TQ, TQ), TQ)
    q_t = (proj(OFF_Q) * (QK_DIM ** -0.5 * math.log2(math.e))).T
    k = proj(OFF_K).astype(BF16)
    v_t = proj(OFF_V).T
    ones = jnp.ones((V_EXT - V_DIM, TQ), BF16)
    for hd in range(HEADS):
        vt_s[i, hd, 0:V_DIM, :] = v_t[hd * V_DIM:(hd + 1) * V_DIM].astype(BF16)
        vt_s[i, hd, V_DIM:V_EXT, :] = ones
    first_map = lax.broadcasted_iota(jnp.int32, (LANES, TQ), 0) < QK_DIM
    zero = jnp.zeros((LANES, TQ), BF16)
    for hd in range(HEADS):
        cols = slice(hd * LANES, (hd + 1) * LANES)
        q_hd = q_t[cols].astype(BF16)
        qt_s[hd, :, 0:TQ] = jnp.where(first_map, q_hd, zero)
        qt_s[hd, :, TQ:2 * TQ] = jnp.where(first_map, zero, q_hd)
        k_s[hd, rows_i, :] = k[:, cols]

    col_strips = [slice(c, c + COL_STRIP) for c in range(0, 2 * TQ, COL_STRIP)]

    def visible_keys(cs):
        return cs.start % TQ + COL_STRIP

    def diag_probs(hd):
        probs = []
        for cs in col_strips:
            nk = visible_keys(cs)
            keys = k_s[hd, pl.ds(pl.multiple_of(i * TQ, TQ), nk), :]
            s = _dot(keys, qt_s[hd, :, cs])
            key_chunk = lax.shift_right_logical(
                lax.broadcasted_iota(jnp.int32, (nk, COL_STRIP), 0), CHUNK_SHIFT)
            qry_chunk = lax.shift_right_logical(
                lax.broadcasted_iota(jnp.int32, (nk, COL_STRIP), 1) + cs.start % TQ, CHUNK_SHIFT)
            s = jnp.where(key_chunk <= qry_chunk, s, -jnp.inf)
            mx = jnp.max(s, axis=0, keepdims=True)
            m_s[hd, :, cs] = mx
            probs.append(jnp.exp2(s - mx).astype(BF16))
        return probs

    def diag_values(hd, probs):
        for cs, p in zip(col_strips, probs):
            acc_s[hd, :, cs] = _dot(vt_s[i, hd, :, 0:visible_keys(cs)], p)

    p0 = diag_probs(0)

    su = _gelu(proj(OFF_SU))
    p1 = diag_probs(1)
    sv = _gelu(proj(OFF_SV))
    diag_values(0, p0)
    mu = jnp.mean(sv, axis=-1, keepdims=True)
    cen = sv - mu
    var = jnp.mean(cen * cen, axis=-1, keepdims=True)
    svn = (cen * lax.rsqrt(var + EPS) * row(ln_g_ref) + row(ln_b_ref)).astype(BF16)
    p2 = diag_probs(2)

    ext_s[HALO:HALO + TQ, :] = proj(OFF_P)
    gate_b = _silu_of_half(proj(OFF_GB))
    diag_values(1, p1)
    p3 = diag_probs(3)
    pos = lax.broadcasted_iota(jnp.int32, (TQ, LANES), 0) + (i * TQ + 1)
    mixed = []
    for g, win in enumerate(POOL_WINDOWS):
        cols = slice(g * GROUP_DIM, (g + 1) * GROUP_DIM)
        wsum = ext_s[:, cols]
        shift = 1
        while shift < win:
            rows = wsum.shape[0] - SUBLANES
            wsum = wsum[SUBLANES:] + wsum[SUBLANES - shift:SUBLANES - shift + rows]
            shift *= 2
        wsum = wsum[wsum.shape[0] - TQ:]
        u = ext_s[HALO:HALO + TQ, cols]
        count = jnp.minimum(pos, win).astype(F32)
        pooled = (wsum / count - u).astype(BF16)
        mixed.append(_dot(pooled, pool_w_ref[0, g].astype(BF16)) + pool_b_ref[layer, g:g + 1, :])
    y_b = jnp.concatenate(mixed, axis=1) * row(pool_scale_ref)
    ext_s[0:HALO, :] = ext_s[TQ:TQ + HALO, :]
    merge_b = merge_gate(1)
    merged_s[...] = merge_b * _dot((y_b * gate_b).astype(BF16), w_branch_ref[1])

    gate_c = _silu_of_half(proj(OFF_GC))
    diag_values(2, p2)
    sgu_shape = (SGU_BLOCK, SGU_BLOCK)
    pr = lax.shift_right_logical(lax.broadcasted_iota(jnp.int32, sgu_shape, 0), CHUNK_SHIFT)
    pc = lax.shift_right_logical(lax.broadcasted_iota(jnp.int32, sgu_shape, 1), CHUNK_SHIFT)
    sgu_mask = pc <= pr
    blocks = []
    for nb in range(TQ // SGU_BLOCK):
        rws = slice(nb * SGU_BLOCK, (nb + 1) * SGU_BLOCK)
        groups = []
        for g in range(GROUPS):
            cols = slice(g * GROUP_DIM, (g + 1) * GROUP_DIM)
            wg = jnp.where(sgu_mask, sgu_w_ref[0, g], 0.0).astype(BF16)
            bias = jnp.broadcast_to(sgu_bt_ref[0, :, g:g + 1], (SGU_BLOCK, GROUP_DIM))
            groups.append(_dot(wg, svn[rws, cols]) + bias)
        blocks.append(jnp.concatenate(groups, axis=1))
    y_c = su * jnp.concatenate(blocks, axis=0)
    merge_c = merge_gate(2)
    diag_values(3, p3)
    merged_s[...] += merge_c * _dot((y_c * gate_c).astype(BF16), w_branch_ref[2])

    strips = [(hd, cs) for hd in range(HEADS) for cs in col_strips]

    def strip_scores(hd, cs, j, kh):
        rows = pl.ds(pl.multiple_of(j * TQ + kh * KEY_SUB, KEY_SUB), KEY_SUB)
        return _dot(k_s[hd, rows, :], qt_s[hd, :, cs])

    def strip_update(hd, cs, units, scores):
        m_prev = m_s[hd, :, cs]
        m_next = functools.reduce(
            jnp.maximum, [jnp.max(s, axis=0, keepdims=True) for s in scores], m_prev)
        acc = jnp.exp2(m_prev - m_next) * acc_s[hd, :, cs]
        for (j, kh), s in zip(units, scores):
            p = jnp.exp2(s - m_next).astype(BF16)
            acc = acc + _dot(vt_s[j, hd, :, kh * KEY_SUB:(kh + 1) * KEY_SUB], p)
        acc_s[hd, :, cs] = acc
        m_s[hd, :, cs] = m_next

    def key_blocks(js):
        units = [(j, kh) for j in js for kh in range(TQ // KEY_SUB)]
        ahead = max(1, AHEAD // len(units))
        pending = []
        for hd, cs in strips:
            pending.append((hd, cs, [strip_scores(hd, cs, j, kh) for j, kh in units]))
            if len(pending) > ahead:
                hd0, cs0, sc0 = pending.pop(0)
                strip_update(hd0, cs0, units, sc0)
        for hd0, cs0, sc0 in pending:
            strip_update(hd0, cs0, units, sc0)

    def pair_body(jj, c):
        key_blocks((2 * jj, 2 * jj + 1))
        return c

    lax.fori_loop(0, i // 2, pair_body, 0)

    @pl.when(i % 2 == 1)
    def _():
        key_blocks((i - 1,))

    gate_a = _silu_of_half(proj(OFF_GA))
    merge_a = merge_gate(0)
    heads_t = []
    for hd in range(HEADS):
        o = acc_s[hd, 0:V_DIM, :] / acc_s[hd, V_DIM:V_DIM + 1, :]
        o = o[:, :TQ] - lam * o[:, TQ:]
        o = o * lax.rsqrt(jnp.mean(o * o, axis=0, keepdims=True) + EPS) * subln_ref[0]
        heads_t.append(o * (1.0 - lam_init))
    y_a = jnp.concatenate(heads_t, axis=0).T
    merged = merge_a * _dot((y_a * gate_a).astype(BF16), w_branch_ref[0]) + merged_s[...]

    out = _dot(merged.astype(BF16), w_out_ref[...])
    out = out * lax.rsqrt(jnp.mean(out * out, axis=-1, keepdims=True) + EPS) * row(post_g_ref)
    out_ref[0] = x + out


def _full(shape):
    return pl.BlockSpec(shape, lambda b, i: (0,) * len(shape))


def _weight_scales():
    col = np.arange(OFF_GC + WIDTH)
    is_gate = ((col >= OFF_GA) & (col < OFF_P)) | ((col >= OFF_GB) & (col < OFF_SU)) | (col >= OFF_GC)
    return (np.where(is_gate, 0.5, 1.0).astype(np.float32).reshape(1, -1),
            np.full((1, N_BRANCH * D_MODEL), 0.5, np.float32),
            np.ones((1, D_MODEL), np.float32),
            np.full((1, D_MODEL), 0.5, np.float32))


def _layer(x, layer_idx, small, weights, stacked_raw_weights):
    batch, seq, d = x.shape
    assert d == D_MODEL and seq == SEQ and seq % TQ == 0 and TQ % SGU_BLOCK == 0
    lam_init = 0.8 - 0.6 * math.exp(-0.3 * layer_idx)
    w_in, w_merge, w_branch, w_out = weights
    (pre_g, post_g, lq1, lk1, lq2, lk2, subln_g, pool_w, pool_b, pool_scale, ln_g, ln_b, sgu_w,
     sgu_bt, b_merge) = small
    operands = [
        x, pre_g, post_g, w_in, lq1, lk1, lq2, lk2, subln_g, pool_w, pool_b, pool_scale, ln_g,
        ln_b, sgu_w, sgu_bt, w_branch.reshape(N_BRANCH, WIDTH, D_MODEL), w_merge, b_merge, w_out]
    assert len(operands) == N_LAYER_INPUTS
    sliced = {8, 9, 14, 15}
    x_spec = pl.BlockSpec((1, TQ, D_MODEL), lambda b, i: (b, i, 0))

    def of_this_layer(a):
        block = (1,) + a.shape[1:]
        return pl.BlockSpec(block, lambda b, i: (layer_idx,) + (0,) * (len(block) - 1))

    in_specs = [x_spec] + [of_this_layer(a) if n in sliced else _full(a.shape)
                           for n, a in enumerate(operands) if n > 0]
    out_specs = [x_spec]
    out_shape = [jax.ShapeDtypeStruct(x.shape, x.dtype)]
    convert_next = layer_idx + 1 < DEPTH
    if convert_next:
        n_blk = seq // TQ
        n_steps = batch * n_blk
        scales = _weight_scales()
        operands += list(scales) + list(stacked_raw_weights)
        in_specs += [_full(a.shape) for a in scales]
        for w in stacked_raw_weights:
            _, rows, cols = w.shape
            assert rows % (n_steps * 2 * SUBLANES) == 0
            slab = rows // n_steps
            in_specs.append(pl.BlockSpec((1, slab, cols),
                                         lambda b, i: (layer_idx + 1, b * n_blk + i, 0)))
            out_specs.append(pl.BlockSpec((slab, cols), lambda b, i: (b * n_blk + i, 0)))
            out_shape.append(jax.ShapeDtypeStruct((rows, cols), BF16))
    scratch = [
        pltpu.VMEM((HEADS, LANES, 2 * TQ), BF16),
        pltpu.VMEM((HEADS, SEQ, LANES), BF16),
        pltpu.VMEM((SEQ // TQ, HEADS, V_EXT, TQ), BF16),
        pltpu.VMEM((HEADS, 1, 2 * TQ), F32),
        pltpu.VMEM((HEADS, V_EXT, 2 * TQ), F32),
        pltpu.VMEM((TQ + HALO, WIDTH), F32),
        pltpu.VMEM((TQ, D_MODEL), F32),
    ]
    outs = pl.pallas_call(
        functools.partial(_layer_kernel, layer=layer_idx, lam_init=lam_init,
                          convert_next=convert_next),
        grid=(batch, seq // TQ),
        in_specs=in_specs,
        out_specs=out_specs,
        out_shape=out_shape,
        scratch_shapes=scratch,
        compiler_params=pltpu.CompilerParams(
            dimension_semantics=("arbitrary", "arbitrary"),
            vmem_limit_bytes=VMEM_LIMIT_BYTES),
        name=f"hybrid_layer_{layer_idx}",
    )(*operands)
    return outs[0], (tuple(outs[1:]) if convert_next else None)


def kernel(x, pre_norm_g, post_norm_g, w_in, lambda_q1, lambda_k1, lambda_q2, lambda_k2, attn_subln_g, pool_w, pool_b, pool_scale, sgu_ln_g, sgu_ln_b, sgu_w, sgu_b, w_branch, w_merge, b_merge, w_out):
    stacked = (w_in, w_merge, w_branch.reshape(DEPTH, N_BRANCH * WIDTH, D_MODEL), w_out)
    small = (pre_norm_g, post_norm_g, lambda_q1, lambda_k1, lambda_q2, lambda_k2,
             attn_subln_g.reshape(DEPTH, -1, 1), pool_w, pool_b, pool_scale,
             sgu_ln_g, sgu_ln_b, sgu_w, sgu_b.transpose(0, 2, 1), b_merge)
    weights = tuple((w[0] * s).astype(BF16) for w, s in zip(stacked, _weight_scales()))
    for l in range(DEPTH):
        x, weights = _layer(x, l, small, weights, stacked)
    return x
```
